```python
import math
import jax, jax.numpy as jnp
from jax import lax
import numpy as np

D_MODEL = 2048
BATCH = 8
SEQ = 2048
DEPTH = 1

HEAD_DIM = 128
A_Q_HEADS = 8
A_KV_HEADS = 2
A_GROUP = A_Q_HEADS // A_KV_HEADS
A_ROPE_THETA = 10000.0
A_WIDTH = A_Q_HEADS * HEAD_DIM
B_HEADS = 4
B_QK_DIM = 128
B_V_DIM = 2 * B_QK_DIM
B_WIDTH = B_HEADS * B_V_DIM
PARTIAL_ROPE_THETA = 500000.0
PARTIAL_ROPE_DIMS = B_QK_DIM // 4
GRID_W = 64
Q_BLOCK = 128
N_EXPERTS = 16
EXPERT_FF = 4096
CAPACITY_FACTOR = 2
RMS_EPS = 1e-6
LN_EPS = 1e-5
DEEPNORM_ALPHA = (2.0 * DEPTH) ** 0.25
DEEPNORM_BETA = (8.0 * DEPTH) ** -0.25
N_BRANCHES = 2
COL_A_Q = A_Q_HEADS * HEAD_DIM
COL_A_K = A_KV_HEADS * HEAD_DIM
COL_A_V = A_KV_HEADS * HEAD_DIM
COL_B_Q = B_HEADS * 2 * B_QK_DIM
COL_B_K = B_HEADS * 2 * B_QK_DIM
COL_B_V = B_HEADS * B_V_DIM
COL_GATES = N_BRANCHES * D_MODEL
IN_COLS = COL_A_Q + COL_A_K + COL_A_V + COL_B_Q + COL_B_K + COL_B_V + COL_GATES

kernel_name = "hybrid_gqa_diffattn_ec_moe_block"


def rms_norm(x, g):
    xf = x.astype(jnp.float32)
    y = xf * lax.rsqrt(jnp.mean(xf * xf, axis=-1, keepdims=True) + RMS_EPS)
    return (y * g.astype(jnp.float32)).astype(x.dtype)


def layer_norm(x, g, b):
    xf = x.astype(jnp.float32)
    mu = jnp.mean(xf, axis=-1, keepdims=True)
    var = jnp.mean(jnp.square(xf - mu), axis=-1, keepdims=True)
    y = (xf - mu) * lax.rsqrt(var + LN_EPS)
    return (y * g.astype(jnp.float32) + b.astype(jnp.float32)).astype(x.dtype)


def rope_cos_sin(pos, dim, theta):
    inv = theta ** (-jnp.arange(0, dim, 2, dtype=jnp.float32) / dim)
    ang = pos.astype(jnp.float32)[:, None] * inv[None, :]
    return jnp.cos(ang), jnp.sin(ang)


def rotate_half_rope(x, cos, sin):
    xf = x.astype(jnp.float32)
    x1, x2 = jnp.split(xf, 2, axis=-1)
    out = jnp.concatenate([x1 * cos - x2 * sin, x2 * cos + x1 * sin], axis=-1)
    return out.astype(x.dtype)


def axial_rope(x, row_cs, col_cs):
    half = x.shape[-1] // 2
    xr = rotate_half_rope(x[..., :half], *row_cs)
    xc = rotate_half_rope(x[..., half:], *col_cs)
    return jnp.concatenate([xr, xc], axis=-1)


def partial_rope(x, cs):
    xr = rotate_half_rope(x[..., :PARTIAL_ROPE_DIMS], *cs)
    return jnp.concatenate([xr, x[..., PARTIAL_ROPE_DIMS:]], axis=-1)


def gqa_attention(q, k, v):
    b, hk, g, s, d = q.shape
    nb = s // Q_BLOCK
    qb = jnp.moveaxis(q.reshape(b, hk, g, nb, Q_BLOCK, d), 3, 0)
    scale = d ** -0.5

    def one_block(qblk):
        sc = jnp.einsum('bhgqd,bhkd->bhgqk', qblk, k).astype(jnp.float32) * scale
        p = jax.nn.softmax(sc, axis=-1).astype(v.dtype)
        return jnp.einsum('bhgqk,bhkd->bhgqd', p, v)

    o = lax.map(one_block, qb)
    return jnp.moveaxis(o, 0, 3).reshape(b, hk, g, s, d)


def diff_attention(q, k, v, lam):
    b, h, _, s, d = q.shape
    nb = s // Q_BLOCK
    qb = jnp.moveaxis(q.reshape(b, h, 2, nb, Q_BLOCK, d), 3, 0)
    scale = d ** -0.5

    def one_block(qblk):
        sc = jnp.einsum('bhcqd,bhckd->bhcqk', qblk, k).astype(jnp.float32) * scale
        p = jax.nn.softmax(sc, axis=-1)
        a = p[:, :, 0] - lam * p[:, :, 1]
        return jnp.einsum('bhqk,bhkd->bhqd', a.astype(v.dtype), v)

    o = lax.map(one_block, qb)
    return jnp.moveaxis(o, 0, 2).reshape(b, h, s, v.shape[-1])


def expert_choice_ffn(x, w_router, w_gate, w_up, w_down):
    b, s, d = x.shape
    cap = CAPACITY_FACTOR * s // N_EXPERTS
    logits = jnp.einsum('bsd,de->bse', x, w_router).astype(jnp.float32)
    aff = jax.nn.softmax(logits, axis=-1)
    gate, idx = lax.top_k(jnp.swapaxes(aff, 1, 2), cap)
    xg = jax.vmap(lambda xb, ib: xb[ib])(x, idx)
    hid = jax.nn.silu(jnp.einsum('becd,edf->becf', xg, w_gate)) * jnp.einsum('becd,edf->becf', xg, w_up)
    out = jnp.einsum('becf,efd->becd', hid, w_down) * gate[..., None].astype(x.dtype)
    y = jax.vmap(lambda ib, ob: jnp.zeros((s, d), x.dtype).at[ib.reshape(-1)].add(ob.reshape(-1, d)))(idx, out)
    return y


def setup_inputs(seed: int = 0) -> dict:
    key = jax.random.key(seed)
    ks = jax.random.split(key, 20)
    f32 = jnp.float32
    nrm = lambda k, shape, scale: jax.random.normal(k, shape, f32) * scale
    gain = lambda k, shape: 1.0 + 0.02 * jax.random.normal(k, shape, f32)
    return {
        "x": jax.random.normal(ks[0], (BATCH, SEQ, D_MODEL), f32),
        "w_in": nrm(ks[1], (DEPTH, D_MODEL, IN_COLS), D_MODEL ** -0.5),
        "b_gate": nrm(ks[2], (DEPTH, COL_GATES), 0.02),
        "a_q_norm": gain(ks[3], (DEPTH, HEAD_DIM)),
        "a_k_norm": gain(ks[4], (DEPTH, HEAD_DIM)),
        "b_lambda": nrm(ks[5], (DEPTH, 4, B_QK_DIM), 0.1),
        "b_subln": gain(ks[6], (DEPTH, B_V_DIM)),
        "w_a_proj": nrm(ks[7], (DEPTH, A_WIDTH, D_MODEL), A_WIDTH ** -0.5 * DEEPNORM_BETA),
        "w_b_proj": nrm(ks[8], (DEPTH, B_WIDTH, D_MODEL), B_WIDTH ** -0.5 * DEEPNORM_BETA),
        "w_o": nrm(ks[9], (DEPTH, D_MODEL, D_MODEL), D_MODEL ** -0.5 * DEEPNORM_BETA),
        "ln1_g": gain(ks[10], (DEPTH, D_MODEL)),
        "ln1_b": nrm(ks[11], (DEPTH, D_MODEL), 0.02),
        "w_router": nrm(ks[12], (DEPTH, D_MODEL, N_EXPERTS), D_MODEL ** -0.5),
        "w_gate": nrm(ks[13], (DEPTH, N_EXPERTS, D_MODEL, EXPERT_FF), D_MODEL ** -0.5),
        "w_up": nrm(ks[14], (DEPTH, N_EXPERTS, D_MODEL, EXPERT_FF), D_MODEL ** -0.5),
        "w_down": nrm(ks[15], (DEPTH, N_EXPERTS, EXPERT_FF, D_MODEL), EXPERT_FF ** -0.5 * DEEPNORM_BETA),
        "ln2_g": gain(ks[16], (DEPTH, D_MODEL)),
        "ln2_b": nrm(ks[17], (DEPTH, D_MODEL), 0.02),
    }


def reference(x, w_in, b_gate, a_q_norm, a_k_norm, b_lambda, b_subln, w_a_proj, w_b_proj,
              w_o, ln1_g, ln1_b, w_router, w_gate, w_up, w_down, ln2_g, ln2_b):
    bsz, s, _ = x.shape
    rows = s // GRID_W
    row_idx = jnp.broadcast_to(jnp.arange(rows)[:, None], (rows, GRID_W)).reshape(-1)
    col_idx = jnp.broadcast_to(jnp.arange(GRID_W)[None, :], (rows, GRID_W)).reshape(-1)
    row_cs = rope_cos_sin(row_idx, HEAD_DIM // 2, A_ROPE_THETA)
    col_cs = rope_cos_sin(col_idx, HEAD_DIM // 2, A_ROPE_THETA)
    lin_cs = rope_cos_sin(jnp.arange(s), PARTIAL_ROPE_DIMS, PARTIAL_ROPE_THETA)
    offsets = list(np.cumsum([COL_A_Q, COL_A_K, COL_A_V, COL_B_Q, COL_B_K, COL_B_V]))

    for l in range(DEPTH):
        lam_init = 0.8 - 0.6 * math.exp(-0.3 * l)
        proj = jnp.einsum('bsd,dc->bsc', x, w_in[l])
        qa, ka, va, qb, kb, vb, gates = jnp.split(proj, [int(o) for o in offsets], axis=-1)

        qa = qa.reshape(bsz, s, A_KV_HEADS, A_GROUP, HEAD_DIM).transpose(0, 2, 3, 1, 4)
        ka = ka.reshape(bsz, s, A_KV_HEADS, HEAD_DIM).transpose(0, 2, 1, 3)
        va = va.reshape(bsz, s, A_KV_HEADS, HEAD_DIM).transpose(0, 2, 1, 3)
        qa = axial_rope(rms_norm(qa, a_q_norm[l]), row_cs, col_cs)
        ka = axial_rope(rms_norm(ka, a_k_norm[l]), row_cs, col_cs)
        oa = gqa_attention(qa, ka, va)
        oa = oa.transpose(0, 3, 1, 2, 4).reshape(bsz, s, A_WIDTH)
        ya = jnp.einsum('bsw,wd->bsd', oa, w_a_proj[l])

        qb = partial_rope(qb.reshape(bsz, s, B_HEADS, 2, B_QK_DIM).transpose(0, 2, 3, 1, 4), lin_cs)
        kb = partial_rope(kb.reshape(bsz, s, B_HEADS, 2, B_QK_DIM).transpose(0, 2, 3, 1, 4), lin_cs)
        vb = vb.reshape(bsz, s, B_HEADS, B_V_DIM).transpose(0, 2, 1, 3)
        lp = b_lambda[l].astype(jnp.float32)
        lam = jnp.exp(jnp.sum(lp[0] * lp[1])) - jnp.exp(jnp.sum(lp[2] * lp[3])) + lam_init
        ob = diff_attention(qb, kb, vb, lam)
        ob = rms_norm(ob, b_subln[l]) * (1.0 - lam_init)
        ob = ob.transpose(0, 2, 1, 3).reshape(bsz, s, B_WIDTH)
        yb = jnp.einsum('bsw,wd->bsd', ob, w_b_proj[l])

        g = jax.nn.sigmoid((gates + b_gate[l]).reshape(bsz, s, N_BRANCHES, D_MODEL))
        merged = g[:, :, 0] * ya + g[:, :, 1] * yb
        mix = jnp.einsum('bsd,de->bse', merged, w_o[l])
        x = layer_norm(DEEPNORM_ALPHA * x + mix, ln1_g[l], ln1_b[l])

        ffn = expert_choice_ffn(x, w_router[l], w_gate[l], w_up[l], w_down[l])
        x = layer_norm(DEEPNORM_ALPHA * x + ffn, ln2_g[l], ln2_b[l])
    return x
```

```python
import functools
import math

import jax
import jax.numpy as jnp
from jax import lax
from jax.experimental import pallas as pl
from jax.experimental.pallas import tpu as pltpu

F32 = jnp.float32
BF16 = jnp.bfloat16

HEAD_DIM = 128
A_Q_HEADS = 8
A_KV_HEADS = 2
A_GROUP = A_Q_HEADS // A_KV_HEADS
A_ROPE_THETA = 10000.0
A_WIDTH = A_Q_HEADS * HEAD_DIM
B_HEADS = 4
B_QK_DIM = 128
B_V_DIM = 2 * B_QK_DIM
B_WIDTH = B_HEADS * B_V_DIM
PARTIAL_ROPE_THETA = 500000.0
PARTIAL_ROPE_DIMS = B_QK_DIM // 4
GRID_W = 64
CAPACITY_FACTOR = 2
RMS_EPS = 1e-6
LN_EPS = 1e-5
N_BRANCHES = 2

COL_A_Q = A_Q_HEADS * HEAD_DIM
COL_A_K = A_KV_HEADS * HEAD_DIM
COL_A_V = A_KV_HEADS * HEAD_DIM
COL_B_Q = B_HEADS * 2 * B_QK_DIM
COL_B_K = B_HEADS * 2 * B_QK_DIM
COL_B_V = B_HEADS * B_V_DIM
QKV_COLS = COL_A_Q + COL_A_K + COL_A_V + COL_B_Q + COL_B_K + COL_B_V
OFF_A_K = COL_A_Q
OFF_A_V = OFF_A_K + COL_A_K
OFF_B_Q = OFF_A_V + COL_A_V
OFF_B_K = OFF_B_Q + COL_B_Q
OFF_B_V = OFF_B_K + COL_B_K

LANES = 128
V7X_VMEM_BYTES = 64 * 1024 * 1024
LOG2E = math.log2(math.e)


def _vmem_limit(estimate_bytes):
    return int(min(estimate_bytes * 5 // 4 + (2 << 20), V7X_VMEM_BYTES - (6 << 20)))


def _half_swap(z, half):
    lane = lax.broadcasted_iota(jnp.int32, z.shape, 1)
    return jnp.where((lane & half) == 0, pltpu.roll(z, LANES - half, 1), pltpu.roll(z, half, 1))


def _inproj_kernel(x_ref, w_ref, ca_ref, sa_ref, cb_ref, sb_ref, cs_ref, bias_ref,
                   qkv_ref, g_ref, xb_ref, *, tn, n_qkv):
    j = pl.program_id(1)

    @pl.when(j == 0)
    def _():
        xb_ref[...] = x_ref[...].astype(BF16)

    y = jnp.dot(xb_ref[...], w_ref[...], preferred_element_type=F32)
    nsl = tn // LANES
    a_end = (COL_A_Q + COL_A_K) // tn
    av_end = OFF_B_Q // tn
    bqk_end = OFF_B_V // tn

    @pl.when(j < a_end)
    def _():
        for h in range(nsl):
            sl = slice(h * LANES, (h + 1) * LANES)
            yh = y[:, sl]
            ms = jnp.mean(yh * yh, axis=-1, keepdims=True)
            z = yh * lax.rsqrt(ms + RMS_EPS) * cs_ref[:, sl]
            out = z * ca_ref[...] + _half_swap(z, HEAD_DIM // 4) * sa_ref[...]
            qkv_ref[:, sl] = out.astype(BF16)

    @pl.when(((j >= a_end) & (j < av_end)) | ((j >= bqk_end) & (j < n_qkv)))
    def _():
        qkv_ref[...] = y.astype(BF16)

    @pl.when((j >= av_end) & (j < bqk_end))
    def _():
        for h in range(nsl):
            sl = slice(h * LANES, (h + 1) * LANES)
            z = y[:, sl] * cs_ref[:, sl]
            out = z * cb_ref[...] + _half_swap(z, PARTIAL_ROPE_DIMS // 2) * sb_ref[...]
            qkv_ref[:, sl] = out.astype(BF16)

    @pl.when(j >= n_qkv)
    def _():
        g_ref[...] = 1.0 / (1.0 + jnp.exp(-(y + bias_ref[...])))


def _inproj(x2d, w_bf, tabs, colscale, colbias, seq):
    t, d = x2d.shape
    cols = w_bf.shape[1]
    tm = min(1024, seq)
    tn = 256
    n_qkv = QKV_COLS // tn
    n_g = (cols - QKV_COLS) // tn
    per_seq = seq // tm
    tab_spec = pl.BlockSpec((tm, LANES), lambda i, j: (i % per_seq, 0))
    est = 2 * tm * d * 4 + tm * d * 2 + 2 * d * tn * 2 + 8 * tm * LANES * 4 + 2 * tm * tn * 6 + 3 * tm * tn * 4
    return pl.pallas_call(
        functools.partial(_inproj_kernel, tn=tn, n_qkv=n_qkv),
        grid=(t // tm, n_qkv + n_g),
        in_specs=[
            pl.BlockSpec((tm, d), lambda i, j: (i, 0)),
            pl.BlockSpec((d, tn), lambda i, j: (0, j)),
            tab_spec, tab_spec, tab_spec, tab_spec,
            pl.BlockSpec((1, tn), lambda i, j: (0, j)),
            pl.BlockSpec((1, tn), lambda i, j: (0, j)),
        ],
        out_specs=[
            pl.BlockSpec((tm, tn), lambda i, j: (i, jnp.minimum(j, n_qkv - 1))),
            pl.BlockSpec((tm, tn), lambda i, j: (i, jnp.clip(j - n_qkv, 0, n_g - 1))),
        ],
        out_shape=[
            jax.ShapeDtypeStruct((t, QKV_COLS), BF16),
            jax.ShapeDtypeStruct((t, cols - QKV_COLS), F32),
        ],
        scratch_shapes=[pltpu.VMEM((tm, d), BF16)],
        compiler_params=pltpu.CompilerParams(
            dimension_semantics=("arbitrary", "arbitrary"), vmem_limit_bytes=_vmem_limit(est)),
        name="inproj",
    )(x2d, w_bf, *tabs, colscale, colbias)


def _softmax_parts(q, k):
    s = lax.dot_general(q, k, (((1,), (1,)), ((), ())), preferred_element_type=F32)
    p = jnp.exp2(s - jnp.max(s, axis=-1, keepdims=True))
    return p, jnp.sum(p, axis=-1, keepdims=True)


def _attn_a_kernel(q_ref, k_ref, v_ref, o_ref, *, tq):
    nq = q_ref.shape[0] // tq

    def body(qi, carry):
        r = pl.multiple_of(qi * tq, tq)
        for g in range(A_GROUP):
            sl = slice(g * HEAD_DIM, (g + 1) * HEAD_DIM)
            p, l = _softmax_parts(q_ref[pl.ds(r, tq), sl], k_ref[...])
            o = jnp.dot(p.astype(BF16), v_ref[...], preferred_element_type=F32)
            o_ref[pl.ds(r, tq), sl] = (o / l).astype(BF16)
        return carry

    lax.fori_loop(0, nq, body, 0)


def _attn_a(qkv, bsz, seq):
    t = qkv.shape[0]
    gw = A_GROUP * HEAD_DIM
    tq = min(256, seq)
    est = 2 * (2 * seq * gw * 2 + 2 * seq * HEAD_DIM * 2) + 4 * tq * seq * 4
    return pl.pallas_call(
        functools.partial(_attn_a_kernel, tq=tq),
        grid=(bsz, A_KV_HEADS),
        in_specs=[
            pl.BlockSpec((seq, gw), lambda b, h: (b, h)),
            pl.BlockSpec((seq, HEAD_DIM), lambda b, h: (b, OFF_A_K // HEAD_DIM + h)),
            pl.BlockSpec((seq, HEAD_DIM), lambda b, h: (b, OFF_A_V // HEAD_DIM + h)),
        ],
        out_specs=pl.BlockSpec((seq, gw), lambda b, h: (b, h)),
        out_shape=jax.ShapeDtypeStruct((t, A_WIDTH), BF16),
        compiler_params=pltpu.CompilerParams(
            dimension_semantics=("parallel", "parallel"), vmem_limit_bytes=_vmem_limit(est)),
        name="attn_a",
    )(qkv, qkv, qkv)


def _attn_b_kernel(lam_ref, sub_ref, q_ref, k_ref, v_ref, o_ref, *, tq, lam_init):
    nq = q_ref.shape[0] // tq
    lp = lam_ref[...]
    lam = (jnp.exp(jnp.sum(lp[0:1] * lp[1:2], axis=-1, keepdims=True))
           - jnp.exp(jnp.sum(lp[2:3] * lp[3:4], axis=-1, keepdims=True)) + lam_init)
    d = B_QK_DIM

    def body(qi, carry):
        r = pl.multiple_of(qi * tq, tq)
        p1, l1 = _softmax_parts(q_ref[pl.ds(r, tq), 0:d], k_ref[:, 0:d])
        p2, l2 = _softmax_parts(q_ref[pl.ds(r, tq), d:2 * d], k_ref[:, d:2 * d])
        a = p1 * (1.0 / l1) - p2 * (lam / l2)
        o = jnp.dot(a.astype(BF16), v_ref[...], preferred_element_type=F32)
        ms = jnp.mean(o * o, axis=-1, keepdims=True)
        o = o * lax.rsqrt(ms + RMS_EPS) * sub_ref[...] * (1.0 - lam_init)
        o_ref[pl.ds(r, tq), :] = o.astype(BF16)
        return carry

    lax.fori_loop(0, nq, body, 0)


def _attn_b(qkv, lam_p, subln, bsz, seq, lam_init):
    t = qkv.shape[0]
    tq = min(256, seq)
    w = 2 * B_QK_DIM
    est = 2 * 4 * seq * w * 2 + 8 * tq * seq * 4
    return pl.pallas_call(
        functools.partial(_attn_b_kernel, tq=tq, lam_init=lam_init),
        grid=(bsz, B_HEADS),
        in_specs=[
            pl.BlockSpec((4, B_QK_DIM), lambda b, h: (0, 0)),
            pl.BlockSpec((1, B_V_DIM), lambda b, h: (0, 0)),
            pl.BlockSpec((seq, w), lambda b, h: (b, OFF_B_Q // w + h)),
            pl.BlockSpec((seq, w), lambda b, h: (b, OFF_B_K // w + h)),
            pl.BlockSpec((seq, B_V_DIM), lambda b, h: (b, OFF_B_V // B_V_DIM + h)),
        ],
        out_specs=pl.BlockSpec((seq, B_V_DIM), lambda b, h: (b, h)),
        out_shape=jax.ShapeDtypeStruct((t, B_WIDTH), BF16),
        compiler_params=pltpu.CompilerParams(
            dimension_semantics=("parallel", "parallel"), vmem_limit_bytes=_vmem_limit(est)),
        name="attn_b",
    )(lam_p, subln, qkv, qkv, qkv)


def _layer_norm(v, g, b):
    mu = jnp.mean(v, axis=-1, keepdims=True)
    c = v - mu
    var = jnp.mean(c * c, axis=-1, keepdims=True)
    return c * lax.rsqrt(var + LN_EPS) * g + b


def _mix_kernel(oa_ref, ob_ref, g_ref, x_ref, wa_ref, wb_ref, wo_ref, lg_ref, lb_ref, wr_ref,
                x1_ref, x1b_ref, aff_ref, *, alpha):
    d = x_ref.shape[1]
    ya = jnp.dot(oa_ref[...], wa_ref[...], preferred_element_type=F32)
    yb = jnp.dot(ob_ref[...], wb_ref[...], preferred_element_type=F32)
    merged = g_ref[:, 0:d] * ya + g_ref[:, d:2 * d] * yb
    mix = jnp.dot(merged.astype(BF16), wo_ref[...], preferred_element_type=F32)
    x1 = _layer_norm(alpha * x_ref[...] + mix, lg_ref[...], lb_ref[...])
    x1_ref[...] = x1
    x1b_ref[...] = x1.astype(BF16)
    logits = jnp.dot(x1, wr_ref[...], preferred_element_type=F32, precision=lax.Precision.HIGHEST)
    e = jnp.exp(logits - jnp.max(logits, axis=-1, keepdims=True))
    aff_ref[...] = e / jnp.sum(e, axis=-1, keepdims=True)


def _mix(oa, ob, g, x2d, wa, wb, wo, ln_g, ln_b, w_router, alpha):
    t, d = x2d.shape
    ne = w_router.shape[1]
    tm = min(256, t)
    const = lambda shape: pl.BlockSpec(shape, lambda i: (0, 0), pipeline_mode=pl.Buffered(1))
    row = lambda width: pl.BlockSpec((tm, width), lambda i: (i, 0))
    est = ((A_WIDTH + B_WIDTH + d) * d * 2 + 2 * tm * (A_WIDTH + B_WIDTH) * 2 + 2 * tm * 2 * d * 4
           + 2 * tm * d * 4 + 2 * tm * d * 6 + 6 * tm * d * 4 + d * LANES * 4)
    return pl.pallas_call(
        functools.partial(_mix_kernel, alpha=alpha),
        grid=(t // tm,),
        in_specs=[
            row(A_WIDTH), row(B_WIDTH), row(2 * d), row(d),
            const((A_WIDTH, d)), const((B_WIDTH, d)), const((d, d)),
            const((1, d)), const((1, d)), const((d, ne)),
        ],
        out_specs=[row(d), row(d), row(ne)],
        out_shape=[
            jax.ShapeDtypeStruct((t, d), F32),
            jax.ShapeDtypeStruct((t, d), BF16),
            jax.ShapeDtypeStruct((t, ne), F32),
        ],
        compiler_params=pltpu.CompilerParams(
            dimension_semantics=("parallel",), vmem_limit_bytes=_vmem_limit(est)),
        name="mix",
    )(oa, ob, g, x2d, wa, wb, wo, ln_g, ln_b, w_router)


def _exclusive_prefix_chunks(mask_chunks, tri):
    out = []
    offset = None
    for m in mask_chunks:
        mf = m.astype(F32)
        incl = jnp.dot(m.astype(BF16), tri, preferred_element_type=F32)
        excl = incl - mf
        out.append(excl if offset is None else excl + offset)
        total = incl[:, LANES - 1:LANES]
        offset = total if offset is None else offset + total
    return out


def _topk_kernel(aff_ref, pos_ref, *, cap):
    ne, seq = aff_ref.shape
    aff = aff_ref[...]

    def search(i, thr):
        cand = thr | lax.shift_left(jnp.int32(1), 30 - i)
        cnt = jnp.sum((aff >= pltpu.bitcast(cand, F32)).astype(jnp.int32), axis=-1, keepdims=True)
        return jnp.where(cnt >= cap, cand, thr)

    thr = lax.fori_loop(0, 31, search, jnp.zeros((ne, 1), jnp.int32))
    gt = aff >= pltpu.bitcast(thr + 1, F32)
    eq = (aff >= pltpu.bitcast(thr, F32)) & jnp.logical_not(gt)
    need = cap - jnp.sum(gt.astype(jnp.int32), axis=-1, keepdims=True)
    tri = (lax.broadcasted_iota(jnp.int32, (LANES, LANES), 0)
           <= lax.broadcasted_iota(jnp.int32, (LANES, LANES), 1)).astype(BF16)
    chunks = [slice(c * LANES, (c + 1) * LANES) for c in range(seq // LANES)]
    eq_rank = _exclusive_prefix_chunks([eq[:, c] for c in chunks], tri)
    sel = [gt[:, c] | (eq[:, c] & (r < need.astype(F32))) for c, r in zip(chunks, eq_rank)]
    slot = _exclusive_prefix_chunks(sel, tri)
    for c, s, p in zip(chunks, sel, slot):
        pos_ref[:, c] = jnp.where(s, p.astype(jnp.int32), -1)


def _topk(aff_t, cap):
    bsz, ne, seq = aff_t.shape
    return pl.pallas_call(
        functools.partial(_topk_kernel, cap=cap),
        grid=(bsz,),
        in_specs=[pl.BlockSpec((None, ne, seq), lambda b: (b, 0, 0))],
        out_specs=pl.BlockSpec((None, ne, seq), lambda b: (b, 0, 0)),
        out_shape=jax.ShapeDtypeStruct((bsz, ne, seq), jnp.int32),
        compiler_params=pltpu.CompilerParams(dimension_semantics=("parallel",)),
        name="topk",
    )(aff_t)


def _gather_kernel(pos_ref, aff_ref, x_ref, xg_ref, gs_ref, *, cap):
    seq = x_ref.shape[0]
    hit = pos_ref[...] == lax.broadcasted_iota(jnp.int32, (cap, seq), 0)
    xg = jnp.dot(jnp.where(hit, 1.0, 0.0).astype(BF16), x_ref[...], preferred_element_type=F32)
    xg_ref[...] = xg.astype(BF16)
    gs_ref[...] = jnp.sum(jnp.where(hit, aff_ref[...], 0.0), axis=-1, keepdims=True)


def _gather(pos4, aff4, x1b, cap):
    bsz, ne, _, seq = pos4.shape
    d = x1b.shape[1]
    row_spec = pl.BlockSpec((None, None, 1, seq), lambda b, e: (b, e, 0, 0))
    est = 2 * seq * d * 2 + 2 * cap * d * 2 + 4 * cap * seq * 4 + cap * d * 4
    return pl.pallas_call(
        functools.partial(_gather_kernel, cap=cap),
        grid=(bsz, ne),
        in_specs=[row_spec, row_spec, pl.BlockSpec((seq, d), lambda b, e: (b, 0))],
        out_specs=[
            pl.BlockSpec((None, cap, d), lambda b, e: (e, b, 0)),
            pl.BlockSpec((None, cap, 1), lambda b, e: (e, b, 0)),
        ],
        out_shape=[
            jax.ShapeDtypeStruct((ne, bsz * cap, d), BF16),
            jax.ShapeDtypeStruct((ne, bsz * cap, 1), F32),
        ],
        compiler_params=pltpu.CompilerParams(
            dimension_semantics=("parallel", "arbitrary"), vmem_limit_bytes=_vmem_limit(est)),
        name="gather",
    )(pos4, aff4, x1b)


def _experts_kernel(xg_ref, gs_ref, wg_ref, wu_ref, wd_ref, og_ref, acc_ref):
    f = pl.program_id(2)

    @pl.when(f == 0)
    def _():
        acc_ref[...] = jnp.zeros_like(acc_ref)

    x = xg_ref[...]
    hg = jnp.dot(x, wg_ref[...].astype(BF16), preferred_element_type=F32)
    hu = jnp.dot(x, wu_ref[...].astype(BF16), preferred_element_type=F32)
    hid = hg * (1.0 / (1.0 + jnp.exp(-hg))) * hu
    acc_ref[...] += jnp.dot(hid.astype(BF16), wd_ref[...].astype(BF16), preferred_element_type=F32)

    @pl.when(f == pl.num_programs(2) - 1)
    def _():
        og_ref[...] = (acc_ref[...] * gs_ref[...]).astype(BF16)


def _experts(xg, gs, w_gate, w_up, w_down):
    ne, rows, d = xg.shape
    ff = w_gate.shape[2]
    tmx = min(1024, rows)
    tf = min(256, ff)
    est = (2 * tmx * d * 2 + 2 * tmx * LANES * 4 + 2 * 3 * d * tf * 4 + 2 * tmx * d * 2 + tmx * d * 4
           + 3 * d * tf * 2 + 4 * tmx * tf * 4 + tmx * d * 4)
    return pl.pallas_call(
        _experts_kernel,
        grid=(ne, rows // tmx, ff // tf),
        in_specs=[
            pl.BlockSpec((None, tmx, d), lambda e, m, f: (e, m, 0)),
            pl.BlockSpec((None, tmx, 1), lambda e, m, f: (e, m, 0)),
            pl.BlockSpec((None, d, tf), lambda e, m, f: (e, 0, f)),
            pl.BlockSpec((None, d, tf), lambda e, m, f: (e, 0, f)),
            pl.BlockSpec((None, tf, d), lambda e, m, f: (e, f, 0)),
        ],
        out_specs=pl.BlockSpec((None, tmx, d), lambda e, m, f: (e, m, 0)),
        out_shape=jax.ShapeDtypeStruct((ne, rows, d), BF16),
        scratch_shapes=[pltpu.VMEM((tmx, d), F32)],
        compiler_params=pltpu.CompilerParams(
            dimension_semantics=("parallel", "parallel", "arbitrary"), vmem_limit_bytes=_vmem_limit(est)),
        name="experts",
    )(xg, gs, w_gate, w_up, w_down)


def _combine_kernel(pos_ref, og_ref, x1_ref, lg_ref, lb_ref, o_ref, acc_ref, *, cap, alpha):
    e = pl.program_id(2)
    ts = x1_ref.shape[0]

    @pl.when(e == 0)
    def _():
        acc_ref[...] = jnp.zeros_like(acc_ref)

    hit = pos_ref[...] == lax.broadcasted_iota(jnp.int32, (cap, ts), 0)
    onehot = jnp.where(hit, 1.0, 0.0).astype(BF16)
    acc_ref[...] += lax.dot_general(onehot, og_ref[...], (((0,), (0,)), ((), ())),
                                    preferred_element_type=F32)

    @pl.when(e == pl.num_programs(2) - 1)
    def _():
        o_ref[...] = _layer_norm(alpha * x1_ref[...] + acc_ref[...], lg_ref[...], lb_ref[...])


def _combine(pos4, og, x1, ln_g, ln_b, cap, alpha):
    bsz, ne, _, seq = pos4.shape
    t, d = x1.shape
    ts = min(512, seq)
    nth = seq // ts
    est = 2 * cap * d * 2 + 4 * ts * d * 4 + ts * d * 4 + 3 * cap * ts * 4 + 3 * ts * d * 4
    return pl.pallas_call(
        functools.partial(_combine_kernel, cap=cap, alpha=alpha),
        grid=(bsz, nth, ne),
        in_specs=[
            pl.BlockSpec((None, None, 1, ts), lambda b, h, e: (b, e, 0, h)),
            pl.BlockSpec((None, cap, d), lambda b, h, e: (e, b, 0)),
            pl.BlockSpec((ts, d), lambda b, h, e: (b * nth + h, 0)),
            pl.BlockSpec((1, d), lambda b, h, e: (0, 0)),
            pl.BlockSpec((1, d), lambda b, h, e: (0, 0)),
        ],
        out_specs=pl.BlockSpec((ts, d), lambda b, h, e: (b * nth + h, 0)),
        out_shape=jax.ShapeDtypeStruct((t, d), F32),
        scratch_shapes=[pltpu.VMEM((ts, d), F32)],
        compiler_params=pltpu.CompilerParams(
            dimension_semantics=("parallel", "parallel", "arbitrary"), vmem_limit_bytes=_vmem_limit(est)),
        name="combine",
    )(pos4, og, x1, ln_g, ln_b)


def _rope_tables(seq):
    pos = jnp.arange(seq)

    def cos_sin(p, dim, theta):
        inv = theta ** (-jnp.arange(0, dim, 2, dtype=F32) / dim)
        ang = p.astype(F32)[:, None] * inv[None, :]
        return jnp.cos(ang), jnp.sin(ang)

    cr, sr = cos_sin(pos // GRID_W, HEAD_DIM // 2, A_ROPE_THETA)
    cc, sc = cos_sin(pos % GRID_W, HEAD_DIM // 2, A_ROPE_THETA)
    cl, sl = cos_sin(pos, PARTIAL_ROPE_DIMS, PARTIAL_ROPE_THETA)
    rest = B_QK_DIM - PARTIAL_ROPE_DIMS
    cos_a = jnp.concatenate([cr, cr, cc, cc], axis=-1)
    sin_a = jnp.concatenate([-sr, sr, -sc, sc], axis=-1)
    cos_b = jnp.concatenate([cl, cl, jnp.ones((seq, rest), F32)], axis=-1)
    sin_b = jnp.concatenate([-sl, sl, jnp.zeros((seq, rest), F32)], axis=-1)
    return cos_a, sin_a, cos_b, sin_b


def kernel(x, w_in, b_gate, a_q_norm, a_k_norm, b_lambda, b_subln, w_a_proj, w_b_proj, w_o, ln1_g, ln1_b,
           w_router, w_gate, w_up, w_down, ln2_g, ln2_b):
    bsz, seq, d = x.shape
    depth = w_in.shape[0]
    ne = w_router.shape[2]
    cap = CAPACITY_FACTOR * seq // ne
    alpha = (2.0 * depth) ** 0.25
    qscale = HEAD_DIM ** -0.5 * LOG2E
    tabs = _rope_tables(seq)
    ones = lambda n: jnp.ones((n,), F32)

    x2d = x.reshape(bsz * seq, d)
    for l in range(depth):
        lam_init = 0.8 - 0.6 * math.exp(-0.3 * l)
        colscale = jnp.concatenate([
            jnp.tile(a_q_norm[l], A_Q_HEADS) * qscale, jnp.tile(a_k_norm[l], A_KV_HEADS), ones(COL_A_V),
            ones(COL_B_Q) * qscale, ones(COL_B_K + COL_B_V + N_BRANCHES * d)])[None, :]
        colbias = jnp.concatenate([jnp.zeros((QKV_COLS,), F32), b_gate[l]])[None, :]
        qkv, gates = _inproj(x2d, w_in[l].astype(BF16), tabs, colscale, colbias, seq)
        oa = _attn_a(qkv, bsz, seq)
        ob = _attn_b(qkv, b_lambda[l], b_subln[l][None, :], bsz, seq, lam_init)
        x1, x1b, aff = _mix(oa, ob, gates, x2d, w_a_proj[l].astype(BF16), w_b_proj[l].astype(BF16),
                            w_o[l].astype(BF16), ln1_g[l][None, :], ln1_b[l][None, :], w_router[l], alpha)
        aff_t = jnp.swapaxes(aff.reshape(bsz, seq, ne), 1, 2)
        pos = _topk(aff_t, cap)
        pos4 = pos.reshape(bsz, ne, 1, seq)
        xg, gs = _gather(pos4, aff_t.reshape(bsz, ne, 1, seq), x1b, cap)
        og = _experts(xg, gs, w_gate[l], w_up[l], w_down[l])
        x2d = _combine(pos4, og, x1, ln2_g[l][None, :], ln2_b[l][None, :], cap, alpha)
    return x2d.reshape(bsz, seq, d)
```

```python
import functools
import math

import numpy as np

import jax
import jax.numpy as jnp
from jax import lax
from jax.experimental import pallas as pl
from jax.experimental.pallas import tpu as pltpu

F32 = jnp.float32
BF16 = jnp.bfloat16

HEAD_DIM = 128
A_Q_HEADS = 8
A_KV_HEADS = 2
A_GROUP = A_Q_HEADS // A_KV_HEADS
A_ROPE_THETA = 10000.0
A_WIDTH = A_Q_HEADS * HEAD_DIM
B_HEADS = 4
B_QK_DIM = 128
B_V_DIM = 2 * B_QK_DIM
B_WIDTH = B_HEADS * B_V_DIM
PARTIAL_ROPE_THETA = 500000.0
PARTIAL_ROPE_DIMS = B_QK_DIM // 4
GRID_W = 64
CAPACITY_FACTOR = 2
RMS_EPS = 1e-6
LN_EPS = 1e-5
N_BRANCHES = 2

COL_A_Q = A_Q_HEADS * HEAD_DIM
COL_A_K = A_KV_HEADS * HEAD_DIM
COL_A_V = A_KV_HEADS * HEAD_DIM
COL_B_Q = B_HEADS * 2 * B_QK_DIM
COL_B_K = B_HEADS * 2 * B_QK_DIM
COL_B_V = B_HEADS * B_V_DIM
QKV_COLS = COL_A_Q + COL_A_K + COL_A_V + COL_B_Q + COL_B_K + COL_B_V
OFF_A_K = COL_A_Q
OFF_A_V = OFF_A_K + COL_A_K
OFF_B_Q = OFF_A_V + COL_A_V
OFF_B_K = OFF_B_Q + COL_B_Q
OFF_B_V = OFF_B_K + COL_B_K

LANES = 128
V7X_VMEM_BYTES = 64 * 1024 * 1024
LOG2E = math.log2(math.e)


def _vmem_limit(estimate_bytes):
    return int(min(estimate_bytes * 5 // 4 + (2 << 20), V7X_VMEM_BYTES - (6 << 20)))


_QUARTER = HEAD_DIM // 4
_PERM_A = np.concatenate([np.arange(0, _QUARTER), np.arange(2 * _QUARTER, 3 * _QUARTER),
                          np.arange(_QUARTER, 2 * _QUARTER), np.arange(3 * _QUARTER, HEAD_DIM)])
_HALF_B = PARTIAL_ROPE_DIMS // 2
_PERM_B = np.arange(B_QK_DIM)
_PERM_B[_HALF_B:2 * _HALF_B] = np.arange(LANES // 2, LANES // 2 + _HALF_B)
_PERM_B[LANES // 2:LANES // 2 + _HALF_B] = np.arange(_HALF_B, 2 * _HALF_B)


def _column_permutation(n_cols):
    idx = np.arange(n_cols)
    for off in list(range(0, OFF_A_V, HEAD_DIM)):
        idx[off:off + HEAD_DIM] = off + _PERM_A
    for off in list(range(OFF_B_Q, OFF_B_V, B_QK_DIM)):
        idx[off:off + B_QK_DIM] = off + _PERM_B
    return idx


def _rope_tables(seq):
    pos = jnp.arange(seq)

    def cos_sin(p, dim, theta):
        inv = theta ** (-jnp.arange(0, dim, 2, dtype=F32) / dim)
        ang = p.astype(F32)[:, None] * inv[None, :]
        return jnp.cos(ang), jnp.sin(ang)

    cr, sr = cos_sin(pos // GRID_W, HEAD_DIM // 2, A_ROPE_THETA)
    cc, sc = cos_sin(pos % GRID_W, HEAD_DIM // 2, A_ROPE_THETA)
    cl, sl = cos_sin(pos, PARTIAL_ROPE_DIMS, PARTIAL_ROPE_THETA)
    one = jnp.ones((seq, LANES // 2 - _HALF_B), F32)
    zero = jnp.zeros((seq, LANES // 2 - _HALF_B), F32)
    cos_a = jnp.concatenate([cr, cc, cr, cc], axis=-1)
    sin_a = jnp.concatenate([-sr, -sc, sr, sc], axis=-1)
    cos_b = jnp.concatenate([cl, one, cl, one], axis=-1)
    sin_b = jnp.concatenate([-sl, zero, sl, zero], axis=-1)
    return cos_a, sin_a, cos_b, sin_b


def _col_kind(col):
    if col < OFF_A_V:
        return "norm_rope"
    if col < OFF_B_Q:
        return "plain"
    if col < OFF_B_V:
        return "rope"
    if col < QKV_COLS:
        return "plain"
    return "gate"


def _inproj_kernel(x_ref, w_ref, ca_ref, sa_ref, cb_ref, sb_ref, cs_ref, bias_ref,
                   qkv_ref, g_ref, xb_ref, *, tn, tc, n_steps):
    j = pl.program_id(1)

    @pl.when(j == 0)
    def _():
        xb_ref[...] = x_ref[...].astype(BF16)

    def epilogue(c, kind, y):
        cols = slice(c * tc, (c + 1) * tc)
        if kind == "gate":
            g_ref[:, cols] = (1.0 / (1.0 + jnp.exp(-(y + bias_ref[:, cols])))).astype(g_ref.dtype)
        elif kind == "plain":
            qkv_ref[:, cols] = y.astype(BF16)
        else:
            if kind == "norm_rope":
                head = lambda ax: lax.broadcasted_iota(jnp.int32, (tc, tc), ax) // HEAD_DIM
                blockdiag = (head(0) == head(1)).astype(BF16)
                ss = jnp.dot((y * y).astype(BF16), blockdiag, preferred_element_type=F32)
                z = y * lax.rsqrt(ss * (1.0 / HEAD_DIM) + RMS_EPS) * cs_ref[:, cols]
                cos_ref, sin_ref = ca_ref, sa_ref
            else:
                z = y * cs_ref[:, cols]
                cos_ref, sin_ref = cb_ref, sb_ref
            for h in range(tc // LANES):
                zh = z[:, h * LANES:(h + 1) * LANES]
                out = zh * cos_ref[...] + pltpu.roll(zh, LANES // 2, 1) * sin_ref[...]
                qkv_ref[:, c * tc + h * LANES:c * tc + (h + 1) * LANES] = out.astype(BF16)

    kinds = [tuple(_col_kind(s * tn + c * tc) for c in range(tn // tc)) for s in range(n_steps)]
    start = 0
    while start < n_steps:
        stop = start
        while stop < n_steps and kinds[stop] == kinds[start]:
            stop += 1

        @pl.when((j >= start) & (j < stop))
        def _(ks=kinds[start]):
            ys = [jnp.dot(xb_ref[...], w_ref[:, c * tc:(c + 1) * tc], preferred_element_type=F32)
                  for c in range(len(ks))]
            for c, kind in enumerate(ks):
                epilogue(c, kind, ys[c])

        start = stop


def _inproj(x2d, w_bf, tabs, colscale, colbias, seq, tm=1024, tn=512, tc=256):
    t, d = x2d.shape
    cols = w_bf.shape[1]
    tm = min(tm, seq)
    n_qkv = QKV_COLS // tn
    n_g = (cols - QKV_COLS) // tn
    per_seq = seq // tm
    tab_spec = pl.BlockSpec((tm, LANES), lambda i, j: (i % per_seq, 0))
    est = (2 * tm * d * 4 + tm * d * 2 + 2 * d * tn * 2 + 8 * tm * LANES * 4 + 2 * tm * tn * 4
           + 4 * tm * tn * 4)
    return pl.pallas_call(
        functools.partial(_inproj_kernel, tn=tn, tc=tc, n_steps=n_qkv + n_g),
        grid=(t // tm, n_qkv + n_g),
        in_specs=[
            pl.BlockSpec((tm, d), lambda i, j: (i, 0)),
            pl.BlockSpec((d, tn), lambda i, j: (0, j)),
            tab_spec, tab_spec, tab_spec, tab_spec,
            pl.BlockSpec((1, tn), lambda i, j: (0, j)),
            pl.BlockSpec((1, tn), lambda i, j: (0, j)),
        ],
        out_specs=[
            pl.BlockSpec((tm, tn), lambda i, j: (i, jnp.minimum(j, n_qkv - 1))),
            pl.BlockSpec((tm, tn), lambda i, j: (i, jnp.clip(j - n_qkv, 0, n_g - 1))),
        ],
        out_shape=[
            jax.ShapeDtypeStruct((t, QKV_COLS), BF16),
            jax.ShapeDtypeStruct((t, cols - QKV_COLS), BF16),
        ],
        scratch_shapes=[pltpu.VMEM((tm, d), BF16)],
        compiler_params=pltpu.CompilerParams(
            dimension_semantics=("arbitrary", "arbitrary"), vmem_limit_bytes=_vmem_limit(est)),
        name="inproj",
    )(x2d, w_bf, *tabs, colscale, colbias)


def _softmax_parts(q, k):
    s = lax.dot_general(q, k, (((1,), (1,)), ((), ())), preferred_element_type=F32)
    p = jnp.exp2(s - jnp.max(s, axis=-1, keepdims=True))
    return p, jnp.sum(p, axis=-1, keepdims=True)


def _attn_a_kernel(q_ref, k_ref, v_ref, o_ref, *, tq, unroll):
    nq = q_ref.shape[0] // tq

    def body(qi, carry):
        r = pl.multiple_of(qi * tq, tq)
        for g in range(A_GROUP):
            sl = slice(g * HEAD_DIM, (g + 1) * HEAD_DIM)
            p, l = _softmax_parts(q_ref[pl.ds(r, tq), sl], k_ref[...])
            o = jnp.dot(p.astype(BF16), v_ref[...], preferred_element_type=F32)
            o_ref[pl.ds(r, tq), sl] = (o / l).astype(BF16)
        return carry

    lax.fori_loop(0, nq, body, 0, unroll=unroll)


def _attn_a(qkv, bsz, seq, tq=256, unroll=2):
    t = qkv.shape[0]
    gw = A_GROUP * HEAD_DIM
    tq = min(tq, seq)
    unroll = min(unroll, seq // tq)
    est = 2 * (2 * seq * gw * 2 + 2 * seq * HEAD_DIM * 2) + 4 * unroll * tq * seq * 4
    return pl.pallas_call(
        functools.partial(_attn_a_kernel, tq=tq, unroll=unroll),
        grid=(bsz, A_KV_HEADS),
        in_specs=[
            pl.BlockSpec((seq, gw), lambda b, h: (b, h)),
            pl.BlockSpec((seq, HEAD_DIM), lambda b, h: (b, OFF_A_K // HEAD_DIM + h)),
            pl.BlockSpec((seq, HEAD_DIM), lambda b, h: (b, OFF_A_V // HEAD_DIM + h)),
        ],
        out_specs=pl.BlockSpec((seq, gw), lambda b, h: (b, h)),
        out_shape=jax.ShapeDtypeStruct((t, A_WIDTH), BF16),
        compiler_params=pltpu.CompilerParams(
            dimension_semantics=("parallel", "parallel"), vmem_limit_bytes=_vmem_limit(est)),
        name="attn_a",
    )(qkv, qkv, qkv)


def _attn_b_kernel(lam_ref, sub_ref, q_ref, k_ref, v_ref, o_ref, *, tq, lam_init, unroll):
    nq = q_ref.shape[0] // tq
    lp = lam_ref[...]
    lam = (jnp.exp(jnp.sum(lp[0:1] * lp[1:2], axis=-1, keepdims=True))
           - jnp.exp(jnp.sum(lp[2:3] * lp[3:4], axis=-1, keepdims=True)) + lam_init)
    d = B_QK_DIM

    def body(qi, carry):
        r = pl.multiple_of(qi * tq, tq)
        p1, l1 = _softmax_parts(q_ref[pl.ds(r, tq), 0:d], k_ref[:, 0:d])
        p2, l2 = _softmax_parts(q_ref[pl.ds(r, tq), d:2 * d], k_ref[:, d:2 * d])
        a = p1 * (1.0 / l1) - p2 * (lam / l2)
        o = jnp.dot(a.astype(BF16), v_ref[...], preferred_element_type=F32)
        ms = jnp.mean(o * o, axis=-1, keepdims=True)
        o = o * lax.rsqrt(ms + RMS_EPS) * sub_ref[...] * (1.0 - lam_init)
        o_ref[pl.ds(r, tq), :] = o.astype(BF16)
        return carry

    lax.fori_loop(0, nq, body, 0, unroll=unroll)


def _attn_b(qkv, lam_p, subln, bsz, seq, lam_init, tq=256, unroll=4):
    t = qkv.shape[0]
    tq = min(tq, seq)
    unroll = min(unroll, seq // tq)
    w = 2 * B_QK_DIM
    est = 2 * 4 * seq * w * 2 + 5 * unroll * tq * seq * 4
    return pl.pallas_call(
        functools.partial(_attn_b_kernel, tq=tq, lam_init=lam_init, unroll=unroll),
        grid=(bsz, B_HEADS),
        in_specs=[
            pl.BlockSpec((4, B_QK_DIM), lambda b, h: (0, 0)),
            pl.BlockSpec((1, B_V_DIM), lambda b, h: (0, 0)),
            pl.BlockSpec((seq, w), lambda b, h: (b, OFF_B_Q // w + h)),
            pl.BlockSpec((seq, w), lambda b, h: (b, OFF_B_K // w + h)),
            pl.BlockSpec((seq, B_V_DIM), lambda b, h: (b, OFF_B_V // B_V_DIM + h)),
        ],
        out_specs=pl.BlockSpec((seq, B_V_DIM), lambda b, h: (b, h)),
        out_shape=jax.ShapeDtypeStruct((t, B_WIDTH), BF16),
        compiler_params=pltpu.CompilerParams(
            dimension_semantics=("parallel", "parallel"), vmem_limit_bytes=_vmem_limit(est)),
        name="attn_b",
    )(lam_p, subln, qkv, qkv, qkv)


def _layer_norm(v, g, b):
    mu = jnp.mean(v, axis=-1, keepdims=True)
    c = v - mu
    var = jnp.mean(c * c, axis=-1, keepdims=True)
    return c * lax.rsqrt(var + LN_EPS) * g + b


def _merge_kernel(oa_ref, ob_ref, g0_ref, g1_ref, wa_ref, wb_ref, m_ref):
    ya = jnp.dot(oa_ref[...], wa_ref[...], preferred_element_type=F32)
    yb = jnp.dot(ob_ref[...], wb_ref[...], preferred_element_type=F32)
    m_ref[...] = (g0_ref[...].astype(F32) * ya + g1_ref[...].astype(F32) * yb).astype(BF16)


def _merge(oa, ob, g, wa, wb, tm=1024, tn=512):
    t = oa.shape[0]
    d = wa.shape[1]
    tm = min(tm, t)
    nn = d // tn
    est = (2 * 2 * tm * A_WIDTH * 2 + 2 * 2 * A_WIDTH * tn * 2 + 2 * 2 * tm * tn * 2 + 2 * tm * tn * 2
           + 4 * tm * tn * 4)
    return pl.pallas_call(
        _merge_kernel,
        grid=(t // tm, nn),
        in_specs=[
            pl.BlockSpec((tm, A_WIDTH), lambda i, n: (i, 0)),
            pl.BlockSpec((tm, B_WIDTH), lambda i, n: (i, 0)),
            pl.BlockSpec((tm, tn), lambda i, n: (i, n)),
            pl.BlockSpec((tm, tn), lambda i, n: (i, nn + n)),
            pl.BlockSpec((A_WIDTH, tn), lambda i, n: (0, n)),
            pl.BlockSpec((B_WIDTH, tn), lambda i, n: (0, n)),
        ],
        out_specs=pl.BlockSpec((tm, tn), lambda i, n: (i, n)),
        out_shape=jax.ShapeDtypeStruct((t, d), BF16),
        compiler_params=pltpu.CompilerParams(
            dimension_semantics=("parallel", "arbitrary"), vmem_limit_bytes=_vmem_limit(est)),
        name="merge",
    )(oa, ob, g, g, wa, wb)


def _outproj_kernel(m_ref, x_ref, wo_ref, lg_ref, lb_ref, wr_ref, x1_ref, x1b_ref, aff_ref, *, alpha):
    ne = aff_ref.shape[1]
    mix = jnp.dot(m_ref[...], wo_ref[...], preferred_element_type=F32)
    x1 = _layer_norm(alpha * x_ref[...] + mix, lg_ref[...], lb_ref[...])
    x1_ref[...] = x1
    hi = x1.astype(BF16)
    x1b_ref[...] = hi
    lo = (x1 - hi.astype(F32)).astype(BF16)
    both = jnp.dot(hi, wr_ref[...], preferred_element_type=F32)
    low = jnp.dot(lo, wr_ref[:, 0:ne], preferred_element_type=F32)
    logits = both[:, 0:ne] + (both[:, ne:2 * ne] + low)
    e = jnp.exp(logits - jnp.max(logits, axis=-1, keepdims=True))
    aff_ref[...] = e / jnp.sum(e, axis=-1, keepdims=True)


def _outproj(merged, x2d, wo, ln_g, ln_b, wr2, alpha, tm=512):
    t, d = x2d.shape
    ne = wr2.shape[1] // 2
    tm = min(tm, t)
    const = lambda shape: pl.BlockSpec(shape, lambda i: (0, 0), pipeline_mode=pl.Buffered(1))
    row = lambda width: pl.BlockSpec((tm, width), lambda i: (i, 0))
    est = d * d * 2 + 2 * tm * d * 2 + 2 * tm * d * 4 + 2 * tm * d * 6 + 5 * tm * d * 4 + d * LANES * 2
    return pl.pallas_call(
        functools.partial(_outproj_kernel, alpha=alpha),
        grid=(t // tm,),
        in_specs=[row(d), row(d), const((d, d)), const((1, d)), const((1, d)), const((d, 2 * ne))],
        out_specs=[row(d), row(d), row(ne)],
        out_shape=[
            jax.ShapeDtypeStruct((t, d), F32),
            jax.ShapeDtypeStruct((t, d), BF16),
            jax.ShapeDtypeStruct((t, ne), F32),
        ],
        compiler_params=pltpu.CompilerParams(
            dimension_semantics=("parallel",), vmem_limit_bytes=_vmem_limit(est)),
        name="outproj",
    )(merged, x2d, wo, ln_g, ln_b, wr2)


def _exclusive_prefix_chunks(mask_chunks, tri):
    out = []
    offset = None
    for m in mask_chunks:
        mf = m.astype(F32)
        incl = jnp.dot(m.astype(BF16), tri, preferred_element_type=F32)
        excl = incl - mf
        out.append(excl if offset is None else excl + offset)
        total = incl[:, LANES - 1:LANES]
        offset = total if offset is None else offset + total
    return out


def _topk_kernel(aff_ref, pos_ref, *, cap):
    ne, seq = aff_ref.shape
    aff = aff_ref[...]

    def search(i, thr):
        cand = thr | lax.shift_left(jnp.int32(1), 30 - i)
        cnt = jnp.sum((aff >= pltpu.bitcast(cand, F32)).astype(jnp.int32), axis=-1, keepdims=True)
        return jnp.where(cnt >= cap, cand, thr)

    thr = lax.fori_loop(0, 31, search, jnp.zeros((ne, 1), jnp.int32))
    gt = aff >= pltpu.bitcast(thr + 1, F32)
    eq = (aff >= pltpu.bitcast(thr, F32)) & jnp.logical_not(gt)
    need = cap - jnp.sum(gt.astype(jnp.int32), axis=-1, keepdims=True)
    tri = (lax.broadcasted_iota(jnp.int32, (LANES, LANES), 0)
           <= lax.broadcasted_iota(jnp.int32, (LANES, LANES), 1)).astype(BF16)
    chunks = [slice(c * LANES, (c + 1) * LANES) for c in range(seq // LANES)]
    eq_rank = _exclusive_prefix_chunks([eq[:, c] for c in chunks], tri)
    sel = [gt[:, c] | (eq[:, c] & (r < need.astype(F32))) for c, r in zip(chunks, eq_rank)]
    slot = _exclusive_prefix_chunks(sel, tri)
    for c, s, p in zip(chunks, sel, slot):
        pos_ref[:, c] = jnp.where(s, p.astype(jnp.int32), -1)


def _topk(aff_t, cap):
    bsz, ne, seq = aff_t.shape
    return pl.pallas_call(
        functools.partial(_topk_kernel, cap=cap),
        grid=(bsz,),
        in_specs=[pl.BlockSpec((None, ne, seq), lambda b: (b, 0, 0))],
        out_specs=pl.BlockSpec((None, ne, seq), lambda b: (b, 0, 0)),
        out_shape=jax.ShapeDtypeStruct((bsz, ne, seq), jnp.int32),
        compiler_params=pltpu.CompilerParams(dimension_semantics=("parallel",)),
        name="topk",
    )(aff_t)


def _gather_kernel(pos_ref, aff_ref, x_ref, xg_ref, gs_ref, *, cap):
    seq = x_ref.shape[0]
    slot = lax.broadcasted_iota(jnp.int32, (cap, seq), 0)
    for e in range(pos_ref.shape[0]):
        hit = pos_ref[e] == slot
        xg = jnp.dot(jnp.where(hit, 1.0, 0.0).astype(BF16), x_ref[...], preferred_element_type=F32)
        xg_ref[e] = xg.astype(BF16)
        gs_ref[e] = jnp.sum(jnp.where(hit, aff_ref[e], 0.0), axis=-1, keepdims=True)


def _gather(pos4, aff4, x1b, cap, eg=4):
    bsz, ne, _, seq = pos4.shape
    d = x1b.shape[1]
    eg = min(eg, ne)
    row_spec = pl.BlockSpec((None, eg, 1, seq), lambda b, e: (b, e, 0, 0))
    est = 2 * seq * d * 2 + 2 * eg * cap * d * 2 + 4 * cap * seq * 4 + 2 * cap * d * 4
    return pl.pallas_call(
        functools.partial(_gather_kernel, cap=cap),
        grid=(bsz, ne // eg),
        in_specs=[row_spec, row_spec, pl.BlockSpec((seq, d), lambda b, e: (b, 0))],
        out_specs=[
            pl.BlockSpec((eg, cap, d), lambda b, e: (e, b, 0)),
            pl.BlockSpec((eg, cap, 1), lambda b, e: (e, b, 0)),
        ],
        out_shape=[
            jax.ShapeDtypeStruct((ne, bsz * cap, d), BF16),
            jax.ShapeDtypeStruct((ne, bsz * cap, 1), F32),
        ],
        compiler_params=pltpu.CompilerParams(
            dimension_semantics=("parallel", "arbitrary"), vmem_limit_bytes=_vmem_limit(est)),
        name="gather",
    )(pos4, aff4, x1b)


def _experts_kernel(xg_ref, gs_ref, wg_ref, wu_ref, wd_ref, og_ref, acc_ref):
    f = pl.program_id(2)

    @pl.when(f == 0)
    def _():
        acc_ref[...] = jnp.zeros_like(acc_ref)

    x = xg_ref[...]
    hg = jnp.dot(x, wg_ref[...].astype(BF16), preferred_element_type=F32)
    hu = jnp.dot(x, wu_ref[...].astype(BF16), preferred_element_type=F32)
    hid = hg * (1.0 / (1.0 + jnp.exp(-hg))) * hu
    acc_ref[...] += jnp.dot(hid.astype(BF16), wd_ref[...].astype(BF16), preferred_element_type=F32)

    @pl.when(f == pl.num_programs(2) - 1)
    def _():
        og_ref[...] = (acc_ref[...] * gs_ref[...]).astype(BF16)


def _experts(xg, gs, w_gate, w_up, w_down, tmx=1024, tf=256):
    ne, rows, d = xg.shape
    ff = w_gate.shape[2]
    tmx = min(tmx, rows)
    tf = min(tf, ff)
    est = (2 * tmx * d * 2 + 2 * tmx * LANES * 4 + 2 * 3 * d * tf * 4 + 2 * tmx * d * 2 + tmx * d * 4
           + 3 * d * tf * 2 + 4 * tmx * tf * 4 + tmx * d * 4)
    return pl.pallas_call(
        _experts_kernel,
        grid=(ne, rows // tmx, ff // tf),
        in_specs=[
            pl.BlockSpec((None, tmx, d), lambda e, m, f: (e, m, 0)),
            pl.BlockSpec((None, tmx, 1), lambda e, m, f: (e, m, 0)),
            pl.BlockSpec((None, d, tf), lambda e, m, f: (e, 0, f)),
            pl.BlockSpec((None, d, tf), lambda e, m, f: (e, 0, f)),
            pl.BlockSpec((None, tf, d), lambda e, m, f: (e, f, 0)),
        ],
        out_specs=pl.BlockSpec((None, tmx, d), lambda e, m, f: (e, m, 0)),
        out_shape=jax.ShapeDtypeStruct((ne, rows, d), BF16),
        scratch_shapes=[pltpu.VMEM((tmx, d), F32)],
        compiler_params=pltpu.CompilerParams(
            dimension_semantics=("parallel", "parallel", "arbitrary"), vmem_limit_bytes=_vmem_limit(est)),
        name="experts",
    )(xg, gs, w_gate, w_up, w_down)


def _combine_kernel(pos_ref, og_ref, x1_ref, lg_ref, lb_ref, o_ref, *, cap, alpha):
    ts = x1_ref.shape[0]
    slot = lax.broadcasted_iota(jnp.int32, (cap, ts), 0)
    y = None
    for e in range(og_ref.shape[0]):
        onehot = jnp.where(pos_ref[e] == slot, 1.0, 0.0).astype(BF16)
        part = lax.dot_general(onehot, og_ref[e], (((0,), (0,)), ((), ())), preferred_element_type=F32)
        y = part if y is None else y + part
    o_ref[...] = _layer_norm(alpha * x1_ref[...] + y, lg_ref[...], lb_ref[...])


def _combine(pos4, og, x1, ln_g, ln_b, cap, alpha, ts=256):
    bsz, ne, _, seq = pos4.shape
    t, d = x1.shape
    ts = min(ts, seq)
    nth = seq // ts
    est = 2 * ne * cap * d * 2 + 4 * ts * d * 4 + 6 * ts * d * 4 + 4 * cap * ts * 4
    return pl.pallas_call(
        functools.partial(_combine_kernel, cap=cap, alpha=alpha),
        grid=(bsz, nth),
        in_specs=[
            pl.BlockSpec((None, ne, 1, ts), lambda b, h: (b, 0, 0, h)),
            pl.BlockSpec((ne, cap, d), lambda b, h: (0, b, 0)),
            pl.BlockSpec((ts, d), lambda b, h: (b * nth + h, 0)),
            pl.BlockSpec((1, d), lambda b, h: (0, 0)),
            pl.BlockSpec((1, d), lambda b, h: (0, 0)),
        ],
        out_specs=pl.BlockSpec((ts, d), lambda b, h: (b * nth + h, 0)),
        out_shape=jax.ShapeDtypeStruct((t, d), F32),
        compiler_params=pltpu.CompilerParams(
            dimension_semantics=("parallel", "arbitrary"), vmem_limit_bytes=_vmem_limit(est)),
        name="combine",
    )(pos4, og, x1, ln_g, ln_b)


def kernel(x, w_in, b_gate, a_q_norm, a_k_norm, b_lambda, b_subln, w_a_proj, w_b_proj, w_o, ln1_g, ln1_b,
           w_router, w_gate, w_up, w_down, ln2_g, ln2_b):
    bsz, seq, d = x.shape
    depth = w_in.shape[0]
    ne = w_router.shape[2]
    cap = CAPACITY_FACTOR * seq // ne
    alpha = (2.0 * depth) ** 0.25
    qscale = HEAD_DIM ** -0.5 * LOG2E
    tabs = _rope_tables(seq)
    perm = _column_permutation(w_in.shape[2])
    ones = lambda n: jnp.ones((n,), F32)

    x2d = x.reshape(bsz * seq, d)
    for l in range(depth):
        lam_init = 0.8 - 0.6 * math.exp(-0.3 * l)
        colscale = jnp.concatenate([
            jnp.tile(a_q_norm[l][_PERM_A], A_Q_HEADS) * qscale, jnp.tile(a_k_norm[l][_PERM_A], A_KV_HEADS),
            ones(COL_A_V), ones(COL_B_Q) * qscale, ones(COL_B_K + COL_B_V + N_BRANCHES * d)])[None, :]
        colbias = jnp.concatenate([jnp.zeros((QKV_COLS,), F32), b_gate[l]])[None, :]
        qkv, gates = _inproj(x2d, w_in[l][:, perm].astype(BF16), tabs, colscale, colbias, seq)
        oa = _attn_a(qkv, bsz, seq)
        ob = _attn_b(qkv, b_lambda[l], b_subln[l][None, :], bsz, seq, lam_init)
        merged = _merge(oa, ob, gates, w_a_proj[l].astype(BF16), w_b_proj[l].astype(BF16))
        wr_hi = w_router[l].astype(BF16)
        wr_lo = (w_router[l] - wr_hi.astype(F32)).astype(BF16)
        x1, x1b, aff = _outproj(merged, x2d, w_o[l].astype(BF16), ln1_g[l][None, :], ln1_b[l][None, :],
                                jnp.concatenate([wr_hi, wr_lo], axis=1), alpha)
        aff_t = jnp.swapaxes(aff.reshape(bsz, seq, ne), 1, 2)
        pos = _topk(aff_t, cap)
        pos4 = pos.reshape(bsz, ne, 1, seq)
        xg, gs = _gather(pos4, aff_t.reshape(bsz, ne, 1, seq), x1b, cap)
        og = _experts(xg, gs, w_gate[l], w_up[l], w_down[l])
        x2d = _combine(pos4, og, x1, ln2_g[l][None, :], ln2_b[l][None, :], cap, alpha)
    return x2d.reshape(bsz, seq, d)
```

```python
import functools
import math

import numpy as np

import jax
import jax.numpy as jnp
from jax import lax
from jax.experimental import pallas as pl
from jax.experimental.pallas import tpu as pltpu

F32 = jnp.float32
BF16 = jnp.bfloat16

HEAD_DIM = 128
A_Q_HEADS = 8
A_KV_HEADS = 2
A_GROUP = A_Q_HEADS // A_KV_HEADS
A_ROPE_THETA = 10000.0
A_WIDTH = A_Q_HEADS * HEAD_DIM
B_HEADS = 4
B_QK_DIM = 128
B_V_DIM = 2 * B_QK_DIM
B_WIDTH = B_HEADS * B_V_DIM
PARTIAL_ROPE_THETA = 500000.0
PARTIAL_ROPE_DIMS = B_QK_DIM // 4
GRID_W = 64
CAPACITY_FACTOR = 2
RMS_EPS = 1e-6
LN_EPS = 1e-5
N_BRANCHES = 2

COL_A_Q = A_Q_HEADS * HEAD_DIM
COL_A_K = A_KV_HEADS * HEAD_DIM
COL_A_V = A_KV_HEADS * HEAD_DIM
COL_B_Q = B_HEADS * 2 * B_QK_DIM
COL_B_K = B_HEADS * 2 * B_QK_DIM
COL_B_V = B_HEADS * B_V_DIM
QKV_COLS = COL_A_Q + COL_A_K + COL_A_V + COL_B_Q + COL_B_K + COL_B_V
OFF_A_K = COL_A_Q
OFF_A_V = OFF_A_K + COL_A_K
OFF_B_Q = OFF_A_V + COL_A_V
OFF_B_K = OFF_B_Q + COL_B_Q
OFF_B_V = OFF_B_K + COL_B_K

LANES = 128
V7X_VMEM_BYTES = 64 * 1024 * 1024
LOG2E = math.log2(math.e)


def _vmem_limit(estimate_bytes):
    return int(min(estimate_bytes * 5 // 4 + (2 << 20), V7X_VMEM_BYTES - (6 << 20)))


_QUARTER = HEAD_DIM // 4
_PERM_A = np.concatenate([np.arange(0, _QUARTER), np.arange(2 * _QUARTER, 3 * _QUARTER),
                          np.arange(_QUARTER, 2 * _QUARTER), np.arange(3 * _QUARTER, HEAD_DIM)])
_HALF_B = PARTIAL_ROPE_DIMS // 2
_PERM_B = np.arange(B_QK_DIM)
_PERM_B[_HALF_B:2 * _HALF_B] = np.arange(LANES // 2, LANES // 2 + _HALF_B)
_PERM_B[LANES // 2:LANES // 2 + _HALF_B] = np.arange(_HALF_B, 2 * _HALF_B)


def _permute_qk_columns(w):
    d = w.shape[0]

    def reorder(cols, width, block, order):
        blocks = cols.reshape(d, cols.shape[1] // width, width // block, block)
        return jnp.concatenate([blocks[:, :, o:o + 1] for o in order], axis=2).reshape(cols.shape)

    order_a = [int(_PERM_A[i * _QUARTER]) // _QUARTER for i in range(HEAD_DIM // _QUARTER)]
    order_b = [int(_PERM_B[i * _HALF_B]) // _HALF_B for i in range(B_QK_DIM // _HALF_B)]
    return jnp.concatenate([
        reorder(w[:, :OFF_A_V], HEAD_DIM, _QUARTER, order_a), w[:, OFF_A_V:OFF_B_Q],
        reorder(w[:, OFF_B_Q:OFF_B_V], B_QK_DIM, _HALF_B, order_b), w[:, OFF_B_V:]], axis=1)


def _rope_tables(seq):
    pos = jnp.arange(seq)

    def cos_sin(p, dim, theta):
        inv = theta ** (-jnp.arange(0, dim, 2, dtype=F32) / dim)
        ang = p.astype(F32)[:, None] * inv[None, :]
        return jnp.cos(ang), jnp.sin(ang)

    cr, sr = cos_sin(pos // GRID_W, HEAD_DIM // 2, A_ROPE_THETA)
    cc, sc = cos_sin(pos % GRID_W, HEAD_DIM // 2, A_ROPE_THETA)
    cl, sl = cos_sin(pos, PARTIAL_ROPE_DIMS, PARTIAL_ROPE_THETA)
    one = jnp.ones((seq, LANES // 2 - _HALF_B), F32)
    zero = jnp.zeros((seq, LANES // 2 - _HALF_B), F32)
    cos_a = jnp.concatenate([cr, cc, cr, cc], axis=-1)
    sin_a = jnp.concatenate([-sr, -sc, sr, sc], axis=-1)
    cos_b = jnp.concatenate([cl, one, cl, one], axis=-1)
    sin_b = jnp.concatenate([-sl, zero, sl, zero], axis=-1)
    return cos_a, sin_a, cos_b, sin_b


def _col_kind(col):
    if col < OFF_A_V:
        return "norm_rope"
    if col < OFF_B_Q:
        return "plain"
    if col < OFF_B_V:
        return "rope"
    if col < QKV_COLS:
        return "plain"
    return "gate"


def _inproj_kernel(x_ref, w_ref, ca_ref, sa_ref, cb_ref, sb_ref, cs_ref, bias_ref,
                   qkv_ref, g_ref, xb_ref, *, tn, tc, n_steps):
    j = pl.program_id(1)

    @pl.when(j == 0)
    def _():
        xb_ref[...] = x_ref[...].astype(BF16)

    def epilogue(c, kind, y):
        cols = slice(c * tc, (c + 1) * tc)
        if kind == "gate":
            g_ref[:, cols] = (1.0 / (1.0 + jnp.exp(-(y + bias_ref[:, cols])))).astype(g_ref.dtype)
        elif kind == "plain":
            qkv_ref[:, cols] = y.astype(BF16)
        else:
            if kind == "norm_rope":
                head = lambda ax: lax.broadcasted_iota(jnp.int32, (tc, tc), ax) // HEAD_DIM
                blockdiag = (head(0) == head(1)).astype(BF16)
                ss = jnp.dot((y * y).astype(BF16), blockdiag, preferred_element_type=F32)
                z = y * lax.rsqrt(ss * (1.0 / HEAD_DIM) + RMS_EPS) * cs_ref[:, cols]
                cos_ref, sin_ref = ca_ref, sa_ref
            else:
                z = y * cs_ref[:, cols]
                cos_ref, sin_ref = cb_ref, sb_ref
            for h in range(tc // LANES):
                zh = z[:, h * LANES:(h + 1) * LANES]
                out = zh * cos_ref[...] + pltpu.roll(zh, LANES // 2, 1) * sin_ref[...]
                qkv_ref[:, c * tc + h * LANES:c * tc + (h + 1) * LANES] = out.astype(BF16)

    kinds = [tuple(_col_kind(s * tn + c * tc) for c in range(tn // tc)) for s in range(n_steps)]
    start = 0
    while start < n_steps:
        stop = start
        while stop < n_steps and kinds[stop] == kinds[start]:
            stop += 1

        @pl.when((j >= start) & (j < stop))
        def _(ks=kinds[start]):
            ys = [jnp.dot(xb_ref[...], w_ref[:, c * tc:(c + 1) * tc], preferred_element_type=F32)
                  for c in range(len(ks))]
            for c, kind in enumerate(ks):
                epilogue(c, kind, ys[c])

        start = stop


def _inproj(x2d, w_bf, tabs, colscale, colbias, seq, tm=1024, tn=512, tc=256):
    t, d = x2d.shape
    cols = w_bf.shape[1]
    tm = min(tm, seq)
    n_qkv = QKV_COLS // tn
    n_g = (cols - QKV_COLS) // tn
    per_seq = seq // tm
    tab_spec = pl.BlockSpec((tm, LANES), lambda i, j: (i % per_seq, 0))
    est = (2 * tm * d * 4 + tm * d * 2 + 2 * d * tn * 2 + 8 * tm * LANES * 4 + 2 * tm * tn * 4
           + 4 * tm * tn * 4)
    return pl.pallas_call(
        functools.partial(_inproj_kernel, tn=tn, tc=tc, n_steps=n_qkv + n_g),
        grid=(t // tm, n_qkv + n_g),
        in_specs=[
            pl.BlockSpec((tm, d), lambda i, j: (i, 0)),
            pl.BlockSpec((d, tn), lambda i, j: (0, j)),
            tab_spec, tab_spec, tab_spec, tab_spec,
            pl.BlockSpec((1, tn), lambda i, j: (0, j)),
            pl.BlockSpec((1, tn), lambda i, j: (0, j)),
        ],
        out_specs=[
            pl.BlockSpec((tm, tn), lambda i, j: (i, jnp.minimum(j, n_qkv - 1))),
            pl.BlockSpec((tm, tn), lambda i, j: (i, jnp.clip(j - n_qkv, 0, n_g - 1))),
        ],
        out_shape=[
            jax.ShapeDtypeStruct((t, QKV_COLS), BF16),
            jax.ShapeDtypeStruct((t, cols - QKV_COLS), BF16),
        ],
        scratch_shapes=[pltpu.VMEM((tm, d), BF16)],
        compiler_params=pltpu.CompilerParams(
            dimension_semantics=("arbitrary", "arbitrary"), vmem_limit_bytes=_vmem_limit(est)),
        name="inproj",
    )(x2d, w_bf, *tabs, colscale, colbias)


def _softmax_parts(q, k):
    s = lax.dot_general(q, k, (((1,), (1,)), ((), ())), preferred_element_type=F32)
    p = jnp.exp2(s - jnp.max(s, axis=-1, keepdims=True))
    return p, jnp.sum(p, axis=-1, keepdims=True)


def _attn_a_kernel(q_ref, k_ref, v_ref, o_ref, *, tq, unroll):
    nq = q_ref.shape[0] // tq

    def body(qi, carry):
        r = pl.multiple_of(qi * tq, tq)
        for g in range(A_GROUP):
            sl = slice(g * HEAD_DIM, (g + 1) * HEAD_DIM)
            p, l = _softmax_parts(q_ref[pl.ds(r, tq), sl], k_ref[...])
            o = jnp.dot(p.astype(BF16), v_ref[...], preferred_element_type=F32)
            o_ref[pl.ds(r, tq), sl] = (o / l).astype(BF16)
        return carry

    lax.fori_loop(0, nq, body, 0, unroll=unroll)


def _attn_a(qkv, bsz, seq, tq=256, unroll=2):
    t = qkv.shape[0]
    gw = A_GROUP * HEAD_DIM
    tq = min(tq, seq)
    unroll = min(unroll, seq // tq)
    est = 2 * (2 * seq * gw * 2 + 2 * seq * HEAD_DIM * 2) + 4 * unroll * tq * seq * 4
    return pl.pallas_call(
        functools.partial(_attn_a_kernel, tq=tq, unroll=unroll),
        grid=(bsz, A_KV_HEADS),
        in_specs=[
            pl.BlockSpec((seq, gw), lambda b, h: (b, h)),
            pl.BlockSpec((seq, HEAD_DIM), lambda b, h: (b, OFF_A_K // HEAD_DIM + h)),
            pl.BlockSpec((seq, HEAD_DIM), lambda b, h: (b, OFF_A_V // HEAD_DIM + h)),
        ],
        out_specs=pl.BlockSpec((seq, gw), lambda b, h: (b, h)),
        out_shape=jax.ShapeDtypeStruct((t, A_WIDTH), BF16),
        compiler_params=pltpu.CompilerParams(
            dimension_semantics=("parallel", "parallel"), vmem_limit_bytes=_vmem_limit(est)),
        name="attn_a",
    )(qkv, qkv, qkv)


def _attn_b_kernel(lam_ref, sub_ref, q_ref, k_ref, v_ref, o_ref, *, tq, lam_init, unroll):
    nq = q_ref.shape[0] // tq
    lp = lam_ref[...]
    lam = (jnp.exp(jnp.sum(lp[0:1] * lp[1:2], axis=-1, keepdims=True))
           - jnp.exp(jnp.sum(lp[2:3] * lp[3:4], axis=-1, keepdims=True)) + lam_init)
    d = B_QK_DIM

    def body(qi, carry):
        r = pl.multiple_of(qi * tq, tq)
        p1, l1 = _softmax_parts(q_ref[pl.ds(r, tq), 0:d], k_ref[:, 0:d])
        p2, l2 = _softmax_parts(q_ref[pl.ds(r, tq), d:2 * d], k_ref[:, d:2 * d])
        o1 = jnp.dot(p1.astype(BF16), v_ref[...], preferred_element_type=F32)
        o2 = jnp.dot(p2.astype(BF16), v_ref[...], preferred_element_type=F32)
        o = o1 * (1.0 / l1) - o2 * (lam / l2)
        ms = jnp.mean(o * o, axis=-1, keepdims=True)
        o = o * lax.rsqrt(ms + RMS_EPS) * sub_ref[...] * (1.0 - lam_init)
        o_ref[pl.ds(r, tq), :] = o.astype(BF16)
        return carry

    lax.fori_loop(0, nq, body, 0, unroll=unroll)


def _attn_b(qkv, lam_p, subln, bsz, seq, lam_init, tq=256, unroll=4):
    t = qkv.shape[0]
    tq = min(tq, seq)
    unroll = min(unroll, seq // tq)
    w = 2 * B_QK_DIM
    est = 2 * 4 * seq * w * 2 + 5 * unroll * tq * seq * 4
    return pl.pallas_call(
        functools.partial(_attn_b_kernel, tq=tq, lam_init=lam_init, unroll=unroll),
        grid=(bsz, B_HEADS),
        in_specs=[
            pl.BlockSpec((4, B_QK_DIM), lambda b, h: (0, 0)),
            pl.BlockSpec((1, B_V_DIM), lambda b, h: (0, 0)),
            pl.BlockSpec((seq, w), lambda b, h: (b, OFF_B_Q // w + h)),
            pl.BlockSpec((seq, w), lambda b, h: (b, OFF_B_K // w + h)),
            pl.BlockSpec((seq, B_V_DIM), lambda b, h: (b, OFF_B_V // B_V_DIM + h)),
        ],
        out_specs=pl.BlockSpec((seq, B_V_DIM), lambda b, h: (b, h)),
        out_shape=jax.ShapeDtypeStruct((t, B_WIDTH), BF16),
        compiler_params=pltpu.CompilerParams(
            dimension_semantics=("parallel", "parallel"), vmem_limit_bytes=_vmem_limit(est)),
        name="attn_b",
    )(lam_p, subln, qkv, qkv, qkv)


def _layer_norm(v, g, b):
    mu = jnp.mean(v, axis=-1, keepdims=True)
    c = v - mu
    var = jnp.mean(c * c, axis=-1, keepdims=True)
    return c * lax.rsqrt(var + LN_EPS) * g + b


def _merge_kernel(oa_ref, ob_ref, g0_ref, g1_ref, wa_ref, wb_ref, m_ref):
    ya = jnp.dot(oa_ref[...], wa_ref[...], preferred_element_type=F32)
    yb = jnp.dot(ob_ref[...], wb_ref[...], preferred_element_type=F32)
    m_ref[...] = (g0_ref[...].astype(F32) * ya + g1_ref[...].astype(F32) * yb).astype(BF16)


def _merge(oa, ob, g, wa, wb, tm=1024, tn=512):
    t = oa.shape[0]
    d = wa.shape[1]
    tm = min(tm, t)
    nn = d // tn
    est = (2 * 2 * tm * A_WIDTH * 2 + 2 * 2 * A_WIDTH * tn * 2 + 2 * 2 * tm * tn * 2 + 2 * tm * tn * 2
           + 4 * tm * tn * 4)
    return pl.pallas_call(
        _merge_kernel,
        grid=(t // tm, nn),
        in_specs=[
            pl.BlockSpec((tm, A_WIDTH), lambda i, n: (i, 0)),
            pl.BlockSpec((tm, B_WIDTH), lambda i, n: (i, 0)),
            pl.BlockSpec((tm, tn), lambda i, n: (i, n)),
            pl.BlockSpec((tm, tn), lambda i, n: (i, nn + n)),
            pl.BlockSpec((A_WIDTH, tn), lambda i, n: (0, n)),
            pl.BlockSpec((B_WIDTH, tn), lambda i, n: (0, n)),
        ],
        out_specs=pl.BlockSpec((tm, tn), lambda i, n: (i, n)),
        out_shape=jax.ShapeDtypeStruct((t, d), BF16),
        compiler_params=pltpu.CompilerParams(
            dimension_semantics=("parallel", "arbitrary"), vmem_limit_bytes=_vmem_limit(est)),
        name="merge",
    )(oa, ob, g, g, wa, wb)


def _outproj_kernel(m_ref, x_ref, wo_ref, lg_ref, lb_ref, wr_ref, x1_ref, x1b_ref, aff_ref, *, alpha):
    ne = aff_ref.shape[1]
    mix = jnp.dot(m_ref[...], wo_ref[...], preferred_element_type=F32)
    x1 = _layer_norm(alpha * x_ref[...] + mix, lg_ref[...], lb_ref[...])
    x1_ref[...] = x1
    hi = x1.astype(BF16)
    x1b_ref[...] = hi
    lo = (x1 - hi.astype(F32)).astype(BF16)
    both = jnp.dot(hi, wr_ref[...], preferred_element_type=F32)
    low = jnp.dot(lo, wr_ref[:, 0:ne], preferred_element_type=F32)
    logits = both[:, 0:ne] + (both[:, ne:2 * ne] + low)
    e = jnp.exp(logits - jnp.max(logits, axis=-1, keepdims=True))
    aff_ref[...] = e / jnp.sum(e, axis=-1, keepdims=True)


def _outproj(merged, x2d, wo, ln_g, ln_b, wr2, alpha, tm=512):
    t, d = x2d.shape
    ne = wr2.shape[1] // 2
    tm = min(tm, t)
    const = lambda shape: pl.BlockSpec(shape, lambda i: (0, 0), pipeline_mode=pl.Buffered(1))
    row = lambda width: pl.BlockSpec((tm, width), lambda i: (i, 0))
    est = d * d * 2 + 2 * tm * d * 2 + 2 * tm * d * 4 + 2 * tm * d * 6 + 5 * tm * d * 4 + d * LANES * 2
    return pl.pallas_call(
        functools.partial(_outproj_kernel, alpha=alpha),
        grid=(t // tm,),
        in_specs=[row(d), row(d), const((d, d)), const((1, d)), const((1, d)), const((d, 2 * ne))],
        out_specs=[row(d), row(d), row(ne)],
        out_shape=[
            jax.ShapeDtypeStruct((t, d), F32),
            jax.ShapeDtypeStruct((t, d), BF16),
            jax.ShapeDtypeStruct((t, ne), F32),
        ],
        compiler_params=pltpu.CompilerParams(
            dimension_semantics=("parallel",), vmem_limit_bytes=_vmem_limit(est)),
        name="outproj",
    )(merged, x2d, wo, ln_g, ln_b, wr2)


def _exclusive_prefix_chunks(mask_chunks, tri):
    out = []
    offset = None
    for m in mask_chunks:
        mf = m.astype(F32)
        incl = jnp.dot(m.astype(BF16), tri, preferred_element_type=F32)
        excl = incl - mf
        out.append(excl if offset is None else excl + offset)
        total = incl[:, LANES - 1:LANES]
        offset = total if offset is None else offset + total
    return out


def _topk_kernel(aff_ref, pos_ref, *, cap):
    ne, seq = aff_ref.shape
    aff = aff_ref[...]

    def search(i, thr):
        cand = thr | lax.shift_left(jnp.int32(1), 30 - i)
        cnt = jnp.sum((aff >= pltpu.bitcast(cand, F32)).astype(jnp.int32), axis=-1, keepdims=True)
        return jnp.where(cnt >= cap, cand, thr)

    thr = lax.fori_loop(0, 31, search, jnp.zeros((ne, 1), jnp.int32))
    gt = aff >= pltpu.bitcast(thr + 1, F32)
    eq = (aff >= pltpu.bitcast(thr, F32)) & jnp.logical_not(gt)
    need = cap - jnp.sum(gt.astype(jnp.int32), axis=-1, keepdims=True)
    tri = (lax.broadcasted_iota(jnp.int32, (LANES, LANES), 0)
           <= lax.broadcasted_iota(jnp.int32, (LANES, LANES), 1)).astype(BF16)
    chunks = [slice(c * LANES, (c + 1) * LANES) for c in range(seq // LANES)]
    eq_rank = _exclusive_prefix_chunks([eq[:, c] for c in chunks], tri)
    sel = [gt[:, c] | (eq[:, c] & (r < need.astype(F32))) for c, r in zip(chunks, eq_rank)]
    slot = _exclusive_prefix_chunks(sel, tri)
    for c, s, p in zip(chunks, sel, slot):
        pos_ref[:, c] = jnp.where(s, p.astype(jnp.int32), -1)


def _topk(aff_t, cap):
    bsz, ne, seq = aff_t.shape
    return pl.pallas_call(
        functools.partial(_topk_kernel, cap=cap),
        grid=(bsz,),
        in_specs=[pl.BlockSpec((None, ne, seq), lambda b: (b, 0, 0))],
        out_specs=pl.BlockSpec((None, ne, seq), lambda b: (b, 0, 0)),
        out_shape=jax.ShapeDtypeStruct((bsz, ne, seq), jnp.int32),
        compiler_params=pltpu.CompilerParams(dimension_semantics=("parallel",)),
        name="topk",
    )(aff_t)


def _gather_kernel(pos_ref, aff_ref, x_ref, xg_ref, gs_ref, *, cap):
    seq = x_ref.shape[0]
    slot = lax.broadcasted_iota(jnp.int32, (cap, seq), 0)
    for e in range(pos_ref.shape[0]):
        hit = pos_ref[e] == slot
        xg = jnp.dot(jnp.where(hit, 1.0, 0.0).astype(BF16), x_ref[...], preferred_element_type=F32)
        xg_ref[e] = xg.astype(BF16)
        gs_ref[e] = jnp.sum(jnp.where(hit, aff_ref[e], 0.0), axis=-1, keepdims=True)


def _gather(pos4, aff4, x1b, cap, eg=4):
    bsz, ne, _, seq = pos4.shape
    d = x1b.shape[1]
    eg = min(eg, ne)
    row_spec = pl.BlockSpec((None, eg, 1, seq), lambda b, e: (b, e, 0, 0))
    est = 2 * seq * d * 2 + 2 * eg * cap * d * 2 + 4 * cap * seq * 4 + 2 * cap * d * 4
    return pl.pallas_call(
        functools.partial(_gather_kernel, cap=cap),
        grid=(bsz, ne // eg),
        in_specs=[row_spec, row_spec, pl.BlockSpec((seq, d), lambda b, e: (b, 0))],
        out_specs=[
            pl.BlockSpec((eg, cap, d), lambda b, e: (e, b, 0)),
            pl.BlockSpec((eg, cap, 1), lambda b, e: (e, b, 0)),
        ],
        out_shape=[
            jax.ShapeDtypeStruct((ne, bsz * cap, d), BF16),
            jax.ShapeDtypeStruct((ne, bsz * cap, 1), F32),
        ],
        compiler_params=pltpu.CompilerParams(
            dimension_semantics=("parallel", "arbitrary"), vmem_limit_bytes=_vmem_limit(est)),
        name="gather",
    )(pos4, aff4, x1b)


def _experts_kernel(xg_ref, gs_ref, wg_ref, wu_ref, wd_ref, og_ref, acc_ref):
    f = pl.program_id(2)

    @pl.when(f == 0)
    def _():
        acc_ref[...] = jnp.zeros_like(acc_ref)

    x = xg_ref[...]
    hg = jnp.dot(x, wg_ref[...].astype(BF16), preferred_element_type=F32)
    hu = jnp.dot(x, wu_ref[...].astype(BF16), preferred_element_type=F32)
    hid = hg * (1.0 / (1.0 + jnp.exp(-hg))) * hu
    acc_ref[...] += jnp.dot(hid.astype(BF16), wd_ref[...].astype(BF16), preferred_element_type=F32)

    @pl.when(f == pl.num_programs(2) - 1)
    def _():
        og_ref[...] = (acc_ref[...] * gs_ref[...]).astype(BF16)


def _experts(xg, gs, w_gate, w_up, w_down, tmx=1024, tf=256):
    ne, rows, d = xg.shape
    ff = w_gate.shape[2]
    tmx = min(tmx, rows)
    tf = min(tf, ff)
    est = (2 * tmx * d * 2 + 2 * tmx * LANES * 4 + 2 * 3 * d * tf * 4 + 2 * tmx * d * 2 + tmx * d * 4
           + 3 * d * tf * 2 + 4 * tmx * tf * 4 + tmx * d * 4)
    return pl.pallas_call(
        _experts_kernel,
        grid=(ne, rows // tmx, ff // tf),
        in_specs=[
            pl.BlockSpec((None, tmx, d), lambda e, m, f: (e, m, 0)),
            pl.BlockSpec((None, tmx, 1), lambda e, m, f: (e, m, 0)),
            pl.BlockSpec((None, d, tf), lambda e, m, f: (e, 0, f)),
            pl.BlockSpec((None, d, tf), lambda e, m, f: (e, 0, f)),
            pl.BlockSpec((None, tf, d), lambda e, m, f: (e, f, 0)),
        ],
        out_specs=pl.BlockSpec((None, tmx, d), lambda e, m, f: (e, m, 0)),
        out_shape=jax.ShapeDtypeStruct((ne, rows, d), BF16),
        scratch_shapes=[pltpu.VMEM((tmx, d), F32)],
        compiler_params=pltpu.CompilerParams(
            dimension_semantics=("parallel", "parallel", "arbitrary"), vmem_limit_bytes=_vmem_limit(est)),
        name="experts",
    )(xg, gs, w_gate, w_up, w_down)


def _combine_kernel(pos_ref, og_ref, x1_ref, lg_ref, lb_ref, o_ref, *, cap, alpha):
    ts = x1_ref.shape[0]
    slot = lax.broadcasted_iota(jnp.int32, (cap, ts), 0)
    y = None
    for e in range(og_ref.shape[0]):
        onehot = jnp.where(pos_ref[e] == slot, 1.0, 0.0).astype(BF16)
        part = lax.dot_general(onehot, og_ref[e], (((0,), (0,)), ((), ())), preferred_element_type=F32)
        y = part if y is None else y + part
    o_ref[...] = _layer_norm(alpha * x1_ref[...] + y, lg_ref[...], lb_ref[...])


def _combine(pos4, og, x1, ln_g, ln_b, cap, alpha, ts=256):
    bsz, ne, _, seq = pos4.shape
    t, d = x1.shape
    ts = min(ts, seq)
    nth = seq // ts
    est = 2 * ne * cap * d * 2 + 4 * ts * d * 4 + 6 * ts * d * 4 + 4 * cap * ts * 4
    return pl.pallas_call(
        functools.partial(_combine_kernel, cap=cap, alpha=alpha),
        grid=(bsz, nth),
        in_specs=[
            pl.BlockSpec((None, ne, 1, ts), lambda b, h: (b, 0, 0, h)),
            pl.BlockSpec((ne, cap, d), lambda b, h: (0, b, 0)),
            pl.BlockSpec((ts, d), lambda b, h: (b * nth + h, 0)),
            pl.BlockSpec((1, d), lambda b, h: (0, 0)),
            pl.BlockSpec((1, d), lambda b, h: (0, 0)),
        ],
        out_specs=pl.BlockSpec((ts, d), lambda b, h: (b * nth + h, 0)),
        out_shape=jax.ShapeDtypeStruct((t, d), F32),
        compiler_params=pltpu.CompilerParams(
            dimension_semantics=("parallel", "arbitrary"), vmem_limit_bytes=_vmem_limit(est)),
        name="combine",
    )(pos4, og, x1, ln_g, ln_b)


def kernel(x, w_in, b_gate, a_q_norm, a_k_norm, b_lambda, b_subln, w_a_proj, w_b_proj, w_o, ln1_g, ln1_b,
           w_router, w_gate, w_up, w_down, ln2_g, ln2_b):
    bsz, seq, d = x.shape
    depth = w_in.shape[0]
    ne = w_router.shape[2]
    cap = CAPACITY_FACTOR * seq // ne
    alpha = (2.0 * depth) ** 0.25
    qscale = HEAD_DIM ** -0.5 * LOG2E
    tabs = _rope_tables(seq)
    ones = lambda n: jnp.ones((n,), F32)

    x2d = x.reshape(bsz * seq, d)
    for l in range(depth):
        lam_init = 0.8 - 0.6 * math.exp(-0.3 * l)
        colscale = jnp.concatenate([
            jnp.tile(a_q_norm[l][_PERM_A], A_Q_HEADS) * qscale, jnp.tile(a_k_norm[l][_PERM_A], A_KV_HEADS),
            ones(COL_A_V), ones(COL_B_Q) * qscale, ones(COL_B_K + COL_B_V + N_BRANCHES * d)])[None, :]
        colbias = jnp.concatenate([jnp.zeros((QKV_COLS,), F32), b_gate[l]])[None, :]
        qkv, gates = _inproj(x2d, _permute_qk_columns(w_in[l]).astype(BF16), tabs, colscale, colbias, seq)
        oa = _attn_a(qkv, bsz, seq)
        ob = _attn_b(qkv, b_lambda[l], b_subln[l][None, :], bsz, seq, lam_init)
        merged = _merge(oa, ob, gates, w_a_proj[l].astype(BF16), w_b_proj[l].astype(BF16))
        wr_hi = w_router[l].astype(BF16)
        wr_lo = (w_router[l] - wr_hi.astype(F32)).astype(BF16)
        x1, x1b, aff = _outproj(merged, x2d, w_o[l].astype(BF16), ln1_g[l][None, :], ln1_b[l][None, :],
                                jnp.concatenate([wr_hi, wr_lo], axis=1), alpha)
        aff_t = jnp.swapaxes(aff.reshape(bsz, seq, ne), 1, 2)
        pos = _topk(aff_t, cap)
        pos4 = pos.reshape(bsz, ne, 1, seq)
        xg, gs = _gather(pos4, aff_t.reshape(bsz, ne, 1, seq), x1b, cap)
        og = _experts(xg, gs, w_gate[l], w_up[l], w_down[l])
        x2d = _combine(pos4, og, x1, ln2_g[l][None, :], ln2_b[l][None, :], cap, alpha)
    return x2d.reshape(bsz, seq, d)
```

```python
import functools
import math

import numpy as np

import jax
import jax.numpy as jnp
from jax import lax
from jax.experimental import pallas as pl
from jax.experimental.pallas import tpu as pltpu

F32 = jnp.float32
BF16 = jnp.bfloat16

HEAD_DIM = 128
A_Q_HEADS = 8
A_KV_HEADS = 2
A_GROUP = A_Q_HEADS // A_KV_HEADS
A_ROPE_THETA = 10000.0
A_WIDTH = A_Q_HEADS * HEAD_DIM
B_HEADS = 4
B_QK_DIM = 128
B_V_DIM = 2 * B_QK_DIM
B_WIDTH = B_HEADS * B_V_DIM
PARTIAL_ROPE_THETA = 500000.0
PARTIAL_ROPE_DIMS = B_QK_DIM // 4
GRID_W = 64
CAPACITY_FACTOR = 2
RMS_EPS = 1e-6
LN_EPS = 1e-5
N_BRANCHES = 2

COL_A_Q = A_Q_HEADS * HEAD_DIM
COL_A_K = A_KV_HEADS * HEAD_DIM
COL_A_V = A_KV_HEADS * HEAD_DIM
COL_B_Q = B_HEADS * 2 * B_QK_DIM
COL_B_K = B_HEADS * 2 * B_QK_DIM
COL_B_V = B_HEADS * B_V_DIM
QKV_COLS = COL_A_Q + COL_A_K + COL_A_V + COL_B_Q + COL_B_K + COL_B_V
OFF_A_K = COL_A_Q
OFF_A_V = OFF_A_K + COL_A_K
OFF_B_Q = OFF_A_V + COL_A_V
OFF_B_K = OFF_B_Q + COL_B_Q
OFF_B_V = OFF_B_K + COL_B_K

LANES = 128
V7X_VMEM_BYTES = 64 * 1024 * 1024
LOG2E = math.log2(math.e)


def _vmem_limit(estimate_bytes):
    return int(min(estimate_bytes * 5 // 4 + (2 << 20), V7X_VMEM_BYTES - (6 << 20)))


_QUARTER = HEAD_DIM // 4
_PERM_A = np.concatenate([np.arange(0, _QUARTER), np.arange(2 * _QUARTER, 3 * _QUARTER),
                          np.arange(_QUARTER, 2 * _QUARTER), np.arange(3 * _QUARTER, HEAD_DIM)])
_HALF_B = PARTIAL_ROPE_DIMS // 2
_PERM_B = np.arange(B_QK_DIM)
_PERM_B[_HALF_B:2 * _HALF_B] = np.arange(LANES // 2, LANES // 2 + _HALF_B)
_PERM_B[LANES // 2:LANES // 2 + _HALF_B] = np.arange(_HALF_B, 2 * _HALF_B)


def _col_kind(col):
    if col < OFF_A_V:
        return "norm_rope"
    if col < OFF_B_Q:
        return "plain"
    if col < OFF_B_V:
        return "rope"
    if col < QKV_COLS:
        return "plain"
    return "gate"


def _runs(kinds):
    out, start = [], 0
    for i in range(1, len(kinds) + 1):
        if i == len(kinds) or kinds[i] != kinds[start]:
            out.append((start, i, kinds[start]))
            start = i
    return out


def _swap_lane_blocks(w, lo, hi, width):
    lane = lax.broadcasted_iota(jnp.int32, w.shape, 1)
    up = pltpu.roll(w, LANES - (hi - lo), 1)
    down = pltpu.roll(w, hi - lo, 1)
    w = jnp.where((lane >= lo) & (lane < lo + width), up, w)
    return jnp.where((lane >= hi) & (lane < hi + width), down, w)


def _prep_w_kernel(w_ref, o_ref, *, tn, n_steps):
    j = pl.program_id(0)
    kinds = [tuple(_col_kind(s * tn + h * LANES) for h in range(tn // LANES)) for s in range(n_steps)]
    for start, stop, ks in _runs(kinds):
        @pl.when((j >= start) & (j < stop))
        def _(ks=ks):
            for h, kind in enumerate(ks):
                sl = slice(h * LANES, (h + 1) * LANES)
                w = w_ref[:, sl]
                if kind == "norm_rope":
                    w = _swap_lane_blocks(w, _QUARTER, 2 * _QUARTER, _QUARTER)
                elif kind == "rope":
                    w = _swap_lane_blocks(w, _HALF_B, LANES // 2, _HALF_B)
                o_ref[:, sl] = w.astype(BF16)


def _prep_w(w, tn=512):
    d, cols = w.shape
    n_steps = cols // tn
    return pl.pallas_call(
        functools.partial(_prep_w_kernel, tn=tn, n_steps=n_steps),
        grid=(n_steps,),
        in_specs=[pl.BlockSpec((d, tn), lambda j: (0, j))],
        out_specs=pl.BlockSpec((d, tn), lambda j: (0, j)),
        out_shape=jax.ShapeDtypeStruct((d, cols), BF16),
        compiler_params=pltpu.CompilerParams(
            dimension_semantics=("parallel",), vmem_limit_bytes=_vmem_limit(2 * d * tn * 6 + 2 * d * tn * 4)),
        name="prep_w",
    )(w)


def _rope_tables(seq):
    pos = jnp.arange(seq)

    def cos_sin(p, dim, theta):
        inv = theta ** (-jnp.arange(0, dim, 2, dtype=F32) / dim)
        ang = p.astype(F32)[:, None] * inv[None, :]
        return jnp.cos(ang), jnp.sin(ang)

    cr, sr = cos_sin(pos // GRID_W, HEAD_DIM // 2, A_ROPE_THETA)
    cc, sc = cos_sin(pos % GRID_W, HEAD_DIM // 2, A_ROPE_THETA)
    cl, sl = cos_sin(pos, PARTIAL_ROPE_DIMS, PARTIAL_ROPE_THETA)
    one = jnp.ones((seq, LANES // 2 - _HALF_B), F32)
    zero = jnp.zeros((seq, LANES // 2 - _HALF_B), F32)
    cos_a = jnp.concatenate([cr, cc, cr, cc], axis=-1)
    sin_a = jnp.concatenate([-sr, -sc, sr, sc], axis=-1)
    cos_b = jnp.concatenate([cl, one, cl, one], axis=-1)
    sin_b = jnp.concatenate([-sl, zero, sl, zero], axis=-1)
    return cos_a, sin_a, cos_b, sin_b


def _inproj_kernel(x_ref, w_ref, ca_ref, sa_ref, cb_ref, sb_ref, cs_ref, bias_ref,
                   qkv_ref, g_ref, xb_ref, *, tn, tc, n_steps):
    j = pl.program_id(1)

    @pl.when(j == 0)
    def _():
        xb_ref[...] = x_ref[...].astype(BF16)

    def epilogue(c, kind, y):
        cols = slice(c * tc, (c + 1) * tc)
        if kind == "gate":
            g_ref[:, cols] = (0.5 * jnp.tanh(0.5 * (y + bias_ref[:, cols])) + 0.5).astype(g_ref.dtype)
        elif kind == "plain":
            qkv_ref[:, cols] = y.astype(BF16)
        else:
            if kind == "norm_rope":
                head = lambda ax: lax.broadcasted_iota(jnp.int32, (tc, tc), ax) // HEAD_DIM
                blockdiag = (head(0) == head(1)).astype(BF16)
                ss = jnp.dot((y * y).astype(BF16), blockdiag, preferred_element_type=F32)
                z = y * lax.rsqrt(ss * (1.0 / HEAD_DIM) + RMS_EPS) * cs_ref[:, cols]
                cos_ref, sin_ref = ca_ref, sa_ref
            else:
                z = y * cs_ref[:, cols]
                cos_ref, sin_ref = cb_ref, sb_ref
            for h in range(tc // LANES):
                zh = z[:, h * LANES:(h + 1) * LANES]
                out = zh * cos_ref[...] + pltpu.roll(zh, LANES // 2, 1) * sin_ref[...]
                qkv_ref[:, c * tc + h * LANES:c * tc + (h + 1) * LANES] = out.astype(BF16)

    kinds = [tuple(_col_kind(s * tn + c * tc) for c in range(tn // tc)) for s in range(n_steps)]
    for start, stop, ks in _runs(kinds):
        @pl.when((j >= start) & (j < stop))
        def _(ks=ks):
            ys = [jnp.dot(xb_ref[...], w_ref[:, c * tc:(c + 1) * tc], preferred_element_type=F32)
                  for c in range(len(ks))]
            for c, kind in enumerate(ks):
                epilogue(c, kind, ys[c])


def _inproj(x2d, w_bf, tabs, colscale, colbias, seq, tm=1024, tn=512, tc=256):
    t, d = x2d.shape
    cols = w_bf.shape[1]
    tm = min(tm, seq)
    n_qkv = QKV_COLS // tn
    n_g = (cols - QKV_COLS) // tn
    per_seq = seq // tm
    tab_spec = pl.BlockSpec((tm, LANES), lambda i, j: (i % per_seq, 0))
    est = (2 * tm * d * 4 + tm * d * 2 + 2 * d * tn * 2 + 8 * tm * LANES * 4 + 2 * tm * tn * 4
           + 4 * tm * tn * 4)
    return pl.pallas_call(
        functools.partial(_inproj_kernel, tn=tn, tc=tc, n_steps=n_qkv + n_g),
        grid=(t // tm, n_qkv + n_g),
        in_specs=[
            pl.BlockSpec((tm, d), lambda i, j: (i, 0)),
            pl.BlockSpec((d, tn), lambda i, j: (0, j)),
            tab_spec, tab_spec, tab_spec, tab_spec,
            pl.BlockSpec((1, tn), lambda i, j: (0, j)),
            pl.BlockSpec((1, tn), lambda i, j: (0, j)),
        ],
        out_specs=[
            pl.BlockSpec((tm, tn), lambda i, j: (i, jnp.minimum(j, n_qkv - 1))),
            pl.BlockSpec((tm, tn), lambda i, j: (i, jnp.clip(j - n_qkv, 0, n_g - 1))),
        ],
        out_shape=[
            jax.ShapeDtypeStruct((t, QKV_COLS), BF16),
            jax.ShapeDtypeStruct((t, cols - QKV_COLS), BF16),
        ],
        scratch_shapes=[pltpu.VMEM((tm, d), BF16)],
        compiler_params=pltpu.CompilerParams(
            dimension_semantics=("arbitrary", "arbitrary"), vmem_limit_bytes=_vmem_limit(est)),
        name="inproj",
    )(x2d, w_bf, *tabs, colscale, colbias)


def _softmax_parts(q, k):
    s = lax.dot_general(q, k, (((1,), (1,)), ((), ())), preferred_element_type=F32)
    p = jnp.exp2(s - jnp.max(s, axis=-1, keepdims=True))
    return p, jnp.sum(p, axis=-1, keepdims=True)


def _attn_a_kernel(q_ref, k_ref, v_ref, o_ref, *, tq, unroll):
    nq = q_ref.shape[0] // tq

    def body(qi, carry):
        r = pl.multiple_of(qi * tq, tq)
        for g in range(A_GROUP):
            sl = slice(g * HEAD_DIM, (g + 1) * HEAD_DIM)
            p, l = _softmax_parts(q_ref[pl.ds(r, tq), sl], k_ref[...])
            o = jnp.dot(p.astype(BF16), v_ref[...], preferred_element_type=F32)
            o_ref[pl.ds(r, tq), sl] = (o / l).astype(BF16)
        return carry

    lax.fori_loop(0, nq, body, 0, unroll=unroll)


def _attn_a(qkv, bsz, seq, tq=256, unroll=2):
    t = qkv.shape[0]
    gw = A_GROUP * HEAD_DIM
    tq = min(tq, seq)
    unroll = min(unroll, seq // tq)
    est = 2 * (2 * seq * gw * 2 + 2 * seq * HEAD_DIM * 2) + 4 * unroll * tq * seq * 4
    return pl.pallas_call(
        functools.partial(_attn_a_kernel, tq=tq, unroll=unroll),
        grid=(bsz, A_KV_HEADS),
        in_specs=[
            pl.BlockSpec((seq, gw), lambda b, h: (b, h)),
            pl.BlockSpec((seq, HEAD_DIM), lambda b, h: (b, OFF_A_K // HEAD_DIM + h)),
            pl.BlockSpec((seq, HEAD_DIM), lambda b, h: (b, OFF_A_V // HEAD_DIM + h)),
        ],
        out_specs=pl.BlockSpec((seq, gw), lambda b, h: (b, h)),
        out_shape=jax.ShapeDtypeStruct((t, A_WIDTH), BF16),
        compiler_params=pltpu.CompilerParams(
            dimension_semantics=("parallel", "parallel"), vmem_limit_bytes=_vmem_limit(est)),
        name="attn_a",
    )(qkv, qkv, qkv)


def _attn_b_kernel(lam_ref, sub_ref, q_ref, k_ref, v_ref, o_ref, *, tq, lam_init, unroll):
    nq = q_ref.shape[0] // tq
    lp = lam_ref[...]
    lam = (jnp.exp(jnp.sum(lp[0:1] * lp[1:2], axis=-1, keepdims=True))
           - jnp.exp(jnp.sum(lp[2:3] * lp[3:4], axis=-1, keepdims=True)) + lam_init)
    d = B_QK_DIM

    def body(qi, carry):
        r = pl.multiple_of(qi * tq, tq)
        p1, l1 = _softmax_parts(q_ref[pl.ds(r, tq), 0:d], k_ref[:, 0:d])
        p2, l2 = _softmax_parts(q_ref[pl.ds(r, tq), d:2 * d], k_ref[:, d:2 * d])
        o1 = jnp.dot(p1.astype(BF16), v_ref[...], preferred_element_type=F32)
        o2 = jnp.dot(p2.astype(BF16), v_ref[...], preferred_element_type=F32)
        o = o1 * (1.0 / l1) - o2 * (lam / l2)
        ms = jnp.mean(o * o, axis=-1, keepdims=True)
        o = o * lax.rsqrt(ms + RMS_EPS) * sub_ref[...] * (1.0 - lam_init)
        o_ref[pl.ds(r, tq), :] = o.astype(BF16)
        return carry

    lax.fori_loop(0, nq, body, 0, unroll=unroll)


def _attn_b(qkv, lam_p, subln, bsz, seq, lam_init, tq=256, unroll=4):
    t = qkv.shape[0]
    tq = min(tq, seq)
    unroll = min(unroll, seq // tq)
    w = 2 * B_QK_DIM
    est = 2 * 4 * seq * w * 2 + 5 * unroll * tq * seq * 4
    return pl.pallas_call(
        functools.partial(_attn_b_kernel, tq=tq, lam_init=lam_init, unroll=unroll),
        grid=(bsz, B_HEADS),
        in_specs=[
            pl.BlockSpec((4, B_QK_DIM), lambda b, h: (0, 0)),
            pl.BlockSpec((1, B_V_DIM), lambda b, h: (0, 0)),
            pl.BlockSpec((seq, w), lambda b, h: (b, OFF_B_Q // w + h)),
            pl.BlockSpec((seq, w), lambda b, h: (b, OFF_B_K // w + h)),
            pl.BlockSpec((seq, B_V_DIM), lambda b, h: (b, OFF_B_V // B_V_DIM + h)),
        ],
        out_specs=pl.BlockSpec((seq, B_V_DIM), lambda b, h: (b, h)),
        out_shape=jax.ShapeDtypeStruct((t, B_WIDTH), BF16),
        compiler_params=pltpu.CompilerParams(
            dimension_semantics=("parallel", "parallel"), vmem_limit_bytes=_vmem_limit(est)),
        name="attn_b",
    )(lam_p, subln, qkv, qkv, qkv)


def _layer_norm(v, g, b):
    mu = jnp.mean(v, axis=-1, keepdims=True)
    c = v - mu
    var = jnp.mean(c * c, axis=-1, keepdims=True)
    return c * lax.rsqrt(var + LN_EPS) * g + b


def _merge_kernel(oa_ref, ob_ref, g0_ref, g1_ref, wa_ref, wb_ref, m_ref):
    ya = jnp.dot(oa_ref[...], wa_ref[...], preferred_element_type=F32)
    yb = jnp.dot(ob_ref[...], wb_ref[...], preferred_element_type=F32)
    m_ref[...] = (g0_ref[...].astype(F32) * ya + g1_ref[...].astype(F32) * yb).astype(BF16)


def _merge(oa, ob, g, wa, wb, tm=1024, tn=512):
    t = oa.shape[0]
    d = wa.shape[1]
    tm = min(tm, t)
    nn = d // tn
    est = (2 * 2 * tm * A_WIDTH * 2 + 2 * 2 * A_WIDTH * tn * 2 + 2 * 2 * tm * tn * 2 + 2 * tm * tn * 2
           + 4 * tm * tn * 4)
    return pl.pallas_call(
        _merge_kernel,
        grid=(t // tm, nn),
        in_specs=[
            pl.BlockSpec((tm, A_WIDTH), lambda i, n: (i, 0)),
            pl.BlockSpec((tm, B_WIDTH), lambda i, n: (i, 0)),
            pl.BlockSpec((tm, tn), lambda i, n: (i, n)),
            pl.BlockSpec((tm, tn), lambda i, n: (i, nn + n)),
            pl.BlockSpec((A_WIDTH, tn), lambda i, n: (0, n)),
            pl.BlockSpec((B_WIDTH, tn), lambda i, n: (0, n)),
        ],
        out_specs=pl.BlockSpec((tm, tn), lambda i, n: (i, n)),
        out_shape=jax.ShapeDtypeStruct((t, d), BF16),
        compiler_params=pltpu.CompilerParams(
            dimension_semantics=("parallel", "arbitrary"), vmem_limit_bytes=_vmem_limit(est)),
        name="merge",
    )(oa, ob, g, g, wa, wb)


def _outproj_kernel(m_ref, x_ref, wo_ref, lg_ref, lb_ref, wr_ref, x1_ref, x1b_ref, aff_ref, *, alpha):
    ne = aff_ref.shape[1]
    mix = jnp.dot(m_ref[...], wo_ref[...], preferred_element_type=F32)
    x1 = _layer_norm(alpha * x_ref[...] + mix, lg_ref[...], lb_ref[...])
    x1_ref[...] = x1
    hi = x1.astype(BF16)
    x1b_ref[...] = hi
    lo = (x1 - hi.astype(F32)).astype(BF16)
    both = jnp.dot(hi, wr_ref[...], preferred_element_type=F32)
    low = jnp.dot(lo, wr_ref[:, 0:ne], preferred_element_type=F32)
    logits = both[:, 0:ne] + (both[:, ne:2 * ne] + low)
    e = jnp.exp(logits - jnp.max(logits, axis=-1, keepdims=True))
    aff_ref[...] = e / jnp.sum(e, axis=-1, keepdims=True)


def _outproj(merged, x2d, wo, ln_g, ln_b, wr2, alpha, tm=512):
    t, d = x2d.shape
    ne = wr2.shape[1] // 2
    tm = min(tm, t)
    const = lambda shape: pl.BlockSpec(shape, lambda i: (0, 0), pipeline_mode=pl.Buffered(1))
    row = lambda width: pl.BlockSpec((tm, width), lambda i: (i, 0))
    est = d * d * 2 + 2 * tm * d * 2 + 2 * tm * d * 4 + 2 * tm * d * 6 + 5 * tm * d * 4 + d * LANES * 2
    return pl.pallas_call(
        functools.partial(_outproj_kernel, alpha=alpha),
        grid=(t // tm,),
        in_specs=[row(d), row(d), const((d, d)), const((1, d)), const((1, d)), const((d, 2 * ne))],
        out_specs=[row(d), row(d), row(ne)],
        out_shape=[
            jax.ShapeDtypeStruct((t, d), F32),
            jax.ShapeDtypeStruct((t, d), BF16),
            jax.ShapeDtypeStruct((t, ne), F32),
        ],
        compiler_params=pltpu.CompilerParams(
            dimension_semantics=("parallel",), vmem_limit_bytes=_vmem_limit(est)),
        name="outproj",
    )(merged, x2d, wo, ln_g, ln_b, wr2)


def _exclusive_prefix_chunks(mask_chunks, tri):
    out = []
    offset = None
    for m in mask_chunks:
        mf = m.astype(F32)
        incl = jnp.dot(m.astype(BF16), tri, preferred_element_type=F32)
        excl = incl - mf
        out.append(excl if offset is None else excl + offset)
        total = incl[:, LANES - 1:LANES]
        offset = total if offset is None else offset + total
    return out


def _topk_kernel(aff_ref, pos_ref, *, cap):
    ne, seq = aff_ref.shape
    aff = aff_ref[...]

    def search(i, thr):
        cand = thr | lax.shift_left(jnp.int32(1), 30 - i)
        cnt = jnp.sum((aff >= pltpu.bitcast(cand, F32)).astype(jnp.int32), axis=-1, keepdims=True)
        return jnp.where(cnt >= cap, cand, thr)

    thr = lax.fori_loop(0, 31, search, jnp.zeros((ne, 1), jnp.int32))
    gt = aff >= pltpu.bitcast(thr + 1, F32)
    eq = (aff >= pltpu.bitcast(thr, F32)) & jnp.logical_not(gt)
    need = cap - jnp.sum(gt.astype(jnp.int32), axis=-1, keepdims=True)
    tri = (lax.broadcasted_iota(jnp.int32, (LANES, LANES), 0)
           <= lax.broadcasted_iota(jnp.int32, (LANES, LANES), 1)).astype(BF16)
    chunks = [slice(c * LANES, (c + 1) * LANES) for c in range(seq // LANES)]
    eq_rank = _exclusive_prefix_chunks([eq[:, c] for c in chunks], tri)
    sel = [gt[:, c] | (eq[:, c] & (r < need.astype(F32))) for c, r in zip(chunks, eq_rank)]
    slot = _exclusive_prefix_chunks(sel, tri)
    for c, s, p in zip(chunks, sel, slot):
        pos_ref[:, c] = jnp.where(s, p.astype(jnp.int32), -1)


def _topk(aff_t, cap):
    bsz, ne, seq = aff_t.shape
    return pl.pallas_call(
        functools.partial(_topk_kernel, cap=cap),
        grid=(bsz,),
        in_specs=[pl.BlockSpec((None, ne, seq), lambda b: (b, 0, 0))],
        out_specs=pl.BlockSpec((None, ne, seq), lambda b: (b, 0, 0)),
        out_shape=jax.ShapeDtypeStruct((bsz, ne, seq), jnp.int32),
        compiler_params=pltpu.CompilerParams(dimension_semantics=("parallel",)),
        name="topk",
    )(aff_t)


def _gather_kernel(pos_ref, aff_ref, x_ref, xg_ref, gs_ref, *, cap):
    seq = x_ref.shape[0]
    slot = lax.broadcasted_iota(jnp.int32, (cap, seq), 0)
    for e in range(pos_ref.shape[0]):
        hit = pos_ref[e] == slot
        xg = jnp.dot(jnp.where(hit, 1.0, 0.0).astype(BF16), x_ref[...], preferred_element_type=F32)
        xg_ref[e] = xg.astype(BF16)
        gs_ref[e] = jnp.sum(jnp.where(hit, aff_ref[e], 0.0), axis=-1, keepdims=True)


def _gather(pos4, aff4, x1b, cap, eg=4):
    bsz, ne, _, seq = pos4.shape
    d = x1b.shape[1]
    eg = min(eg, ne)
    row_spec = pl.BlockSpec((None, eg, 1, seq), lambda b, e: (b, e, 0, 0))
    est = 2 * seq * d * 2 + 2 * eg * cap * d * 2 + 4 * cap * seq * 4 + 2 * cap * d * 4
    return pl.pallas_call(
        functools.partial(_gather_kernel, cap=cap),
        grid=(bsz, ne // eg),
        in_specs=[row_spec, row_spec, pl.BlockSpec((seq, d), lambda b, e: (b, 0))],
        out_specs=[
            pl.BlockSpec((eg, cap, d), lambda b, e: (e, b, 0)),
            pl.BlockSpec((eg, cap, 1), lambda b, e: (e, b, 0)),
        ],
        out_shape=[
            jax.ShapeDtypeStruct((ne, bsz * cap, d), BF16),
            jax.ShapeDtypeStruct((ne, bsz * cap, 1), F32),
        ],
        compiler_params=pltpu.CompilerParams(
            dimension_semantics=("parallel", "arbitrary"), vmem_limit_bytes=_vmem_limit(est)),
        name="gather",
    )(pos4, aff4, x1b)


def _experts_kernel(xg_ref, gs_ref, wg_ref, wu_ref, wd_ref, og_ref, acc_ref):
    f = pl.program_id(2)

    @pl.when(f == 0)
    def _():
        acc_ref[...] = jnp.zeros_like(acc_ref)

    x = xg_ref[...]
    hg = jnp.dot(x, wg_ref[...].astype(BF16), preferred_element_type=F32)
    hu = jnp.dot(x, wu_ref[...].astype(BF16), preferred_element_type=F32)
    hid = hg * (1.0 / (1.0 + jnp.exp(-hg))) * hu
    acc_ref[...] += jnp.dot(hid.astype(BF16), wd_ref[...].astype(BF16), preferred_element_type=F32)

    @pl.when(f == pl.num_programs(2) - 1)
    def _():
        og_ref[...] = (acc_ref[...] * gs_ref[...]).astype(BF16)


def _experts(xg, gs, w_gate, w_up, w_down, tmx=1024, tf=512):
    ne, rows, d = xg.shape
    ff = w_gate.shape[2]
    tmx = min(tmx, rows)
    tf = min(tf, ff)
    est = (2 * tmx * d * 2 + 2 * tmx * LANES * 4 + 2 * 3 * d * tf * 4 + 2 * tmx * d * 2 + tmx * d * 4
           + 3 * d * tf * 2 + 4 * tmx * tf * 4 + tmx * d * 4)
    return pl.pallas_call(
        _experts_kernel,
        grid=(ne, rows // tmx, ff // tf),
        in_specs=[
            pl.BlockSpec((None, tmx, d), lambda e, m, f: (e, m, 0)),
            pl.BlockSpec((None, tmx, 1), lambda e, m, f: (e, m, 0)),
            pl.BlockSpec((None, d, tf), lambda e, m, f: (e, 0, f)),
            pl.BlockSpec((None, d, tf), lambda e, m, f: (e, 0, f)),
            pl.BlockSpec((None, tf, d), lambda e, m, f: (e, f, 0)),
        ],
        out_specs=pl.BlockSpec((None, tmx, d), lambda e, m, f: (e, m, 0)),
        out_shape=jax.ShapeDtypeStruct((ne, rows, d), BF16),
        scratch_shapes=[pltpu.VMEM((tmx, d), F32)],
        compiler_params=pltpu.CompilerParams(
            dimension_semantics=("parallel", "parallel", "arbitrary"), vmem_limit_bytes=_vmem_limit(est)),
        name="experts",
    )(xg, gs, w_gate, w_up, w_down)


def _combine_kernel(pos_ref, og_ref, x1_ref, lg_ref, lb_ref, o_ref, *, cap, alpha):
    ts = x1_ref.shape[0]
    slot = lax.broadcasted_iota(jnp.int32, (cap, ts), 0)
    y = None
    for e in range(og_ref.shape[0]):
        onehot = jnp.where(pos_ref[e] == slot, 1.0, 0.0).astype(BF16)
        part = lax.dot_general(onehot, og_ref[e], (((0,), (0,)), ((), ())), preferred_element_type=F32)
        y = part if y is None else y + part
    o_ref[...] = _layer_norm(alpha * x1_ref[...] + y, lg_ref[...], lb_ref[...])


def _combine(pos4, og, x1, ln_g, ln_b, cap, alpha, ts=256):
    bsz, ne, _, seq = pos4.shape
    t, d = x1.shape
    ts = min(ts, seq)
    nth = seq // ts
    est = 2 * ne * cap * d * 2 + 4 * ts * d * 4 + 6 * ts * d * 4 + 4 * cap * ts * 4
    return pl.pallas_call(
        functools.partial(_combine_kernel, cap=cap, alpha=alpha),
        grid=(bsz, nth),
        in_specs=[
            pl.BlockSpec((None, ne, 1, ts), lambda b, h: (b, 0, 0, h)),
            pl.BlockSpec((ne, cap, d), lambda b, h: (0, b, 0)),
            pl.BlockSpec((ts, d), lambda b, h: (b * nth + h, 0)),
            pl.BlockSpec((1, d), lambda b, h: (0, 0)),
            pl.BlockSpec((1, d), lambda b, h: (0, 0)),
        ],
        out_specs=pl.BlockSpec((ts, d), lambda b, h: (b * nth + h, 0)),
        out_shape=jax.ShapeDtypeStruct((t, d), F32),
        compiler_params=pltpu.CompilerParams(
            dimension_semantics=("parallel", "arbitrary"), vmem_limit_bytes=_vmem_limit(est)),
        name="combine",
    )(pos4, og, x1, ln_g, ln_b)


def kernel(x, w_in, b_gate, a_q_norm, a_k_norm, b_lambda, b_subln, w_a_proj, w_b_proj, w_o, ln1_g, ln1_b,
           w_router, w_gate, w_up, w_down, ln2_g, ln2_b):
    bsz, seq, d = x.shape
    depth = w_in.shape[0]
    ne = w_router.shape[2]
    cap = CAPACITY_FACTOR * seq // ne
    alpha = (2.0 * depth) ** 0.25
    qscale = HEAD_DIM ** -0.5 * LOG2E
    tabs = _rope_tables(seq)
    ones = lambda n: jnp.ones((n,), F32)

    x2d = x.reshape(bsz * seq, d)
    for l in range(depth):
        lam_init = 0.8 - 0.6 * math.exp(-0.3 * l)
        colscale = jnp.concatenate([
            jnp.tile(a_q_norm[l][_PERM_A], A_Q_HEADS) * qscale, jnp.tile(a_k_norm[l][_PERM_A], A_KV_HEADS),
            ones(COL_A_V), ones(COL_B_Q) * qscale, ones(COL_B_K + COL_B_V + N_BRANCHES * d)])[None, :]
        colbias = jnp.concatenate([jnp.zeros((QKV_COLS,), F32), b_gate[l]])[None, :]
        qkv, gates = _inproj(x2d, _prep_w(w_in[l]), tabs, colscale, colbias, seq)
        oa = _attn_a(qkv, bsz, seq)
        ob = _attn_b(qkv, b_lambda[l], b_subln[l][None, :], bsz, seq, lam_init)
        merged = _merge(oa, ob, gates, w_a_proj[l].astype(BF16), w_b_proj[l].astype(BF16))
        wr_hi = w_router[l].astype(BF16)
        wr_lo = (w_router[l] - wr_hi.astype(F32)).astype(BF16)
        x1, x1b, aff = _outproj(merged, x2d, w_o[l].astype(BF16), ln1_g[l][None, :], ln1_b[l][None, :],
                                jnp.concatenate([wr_hi, wr_lo], axis=1), alpha)
        aff_t = jnp.swapaxes(aff.reshape(bsz, seq, ne), 1, 2)
        pos = _topk(aff_t, cap)
        pos4 = pos.reshape(bsz, ne, 1, seq)
        xg, gs = _gather(pos4, aff_t.reshape(bsz, ne, 1, seq), x1b, cap)
        og = _experts(xg, gs, w_gate[l], w_up[l], w_down[l])
        x2d = _combine(pos4, og, x1, ln2_g[l][None, :], ln2_b[l][None, :], cap, alpha)
    return x2d.reshape(bsz, seq, d)
```

```python
import functools
import math

import numpy as np

import jax
import jax.numpy as jnp
from jax import lax
from jax.experimental import pallas as pl
from jax.experimental.pallas import tpu as pltpu

F32 = jnp.float32
BF16 = jnp.bfloat16

HEAD_DIM = 128
A_Q_HEADS = 8
A_KV_HEADS = 2
A_GROUP = A_Q_HEADS // A_KV_HEADS
A_ROPE_THETA = 10000.0
A_WIDTH = A_Q_HEADS * HEAD_DIM
B_HEADS = 4
B_QK_DIM = 128
B_V_DIM = 2 * B_QK_DIM
B_WIDTH = B_HEADS * B_V_DIM
PARTIAL_ROPE_THETA = 500000.0
PARTIAL_ROPE_DIMS = B_QK_DIM // 4
GRID_W = 64
CAPACITY_FACTOR = 2
RMS_EPS = 1e-6
LN_EPS = 1e-5
N_BRANCHES = 2

COL_A_Q = A_Q_HEADS * HEAD_DIM
COL_A_K = A_KV_HEADS * HEAD_DIM
COL_A_V = A_KV_HEADS * HEAD_DIM
COL_B_Q = B_HEADS * 2 * B_QK_DIM
COL_B_K = B_HEADS * 2 * B_QK_DIM
COL_B_V = B_HEADS * B_V_DIM
QKV_COLS = COL_A_Q + COL_A_K + COL_A_V + COL_B_Q + COL_B_K + COL_B_V
OFF_A_K = COL_A_Q
OFF_A_V = OFF_A_K + COL_A_K
OFF_B_Q = OFF_A_V + COL_A_V
OFF_B_K = OFF_B_Q + COL_B_Q
OFF_B_V = OFF_B_K + COL_B_K

LANES = 128
V7X_VMEM_BYTES = 64 * 1024 * 1024
LOG2E = math.log2(math.e)


def _vmem_limit(estimate_bytes):
    return int(min(estimate_bytes * 5 // 4 + (2 << 20), V7X_VMEM_BYTES - (6 << 20)))


_QUARTER = HEAD_DIM // 4
_PERM_A = np.concatenate([np.arange(0, _QUARTER), np.arange(2 * _QUARTER, 3 * _QUARTER),
                          np.arange(_QUARTER, 2 * _QUARTER), np.arange(3 * _QUARTER, HEAD_DIM)])
_HALF_B = PARTIAL_ROPE_DIMS // 2
_PERM_B = np.arange(B_QK_DIM)
_PERM_B[_HALF_B:2 * _HALF_B] = np.arange(LANES // 2, LANES // 2 + _HALF_B)
_PERM_B[LANES // 2:LANES // 2 + _HALF_B] = np.arange(_HALF_B, 2 * _HALF_B)


def _col_kind(col):
    if col < OFF_A_V:
        return "norm_rope"
    if col < OFF_B_Q:
        return "plain"
    if col < OFF_B_V:
        return "rope"
    if col < QKV_COLS:
        return "plain"
    return "gate"


def _runs(kinds):
    out, start = [], 0
    for i in range(1, len(kinds) + 1):
        if i == len(kinds) or kinds[i] != kinds[start]:
            out.append((start, i, kinds[start]))
            start = i
    return out


def _swap_lane_blocks(w, lo, hi, width):
    lane = lax.broadcasted_iota(jnp.int32, w.shape, 1)
    up = pltpu.roll(w, LANES - (hi - lo), 1)
    down = pltpu.roll(w, hi - lo, 1)
    w = jnp.where((lane >= lo) & (lane < lo + width), up, w)
    return jnp.where((lane >= hi) & (lane < hi + width), down, w)


def _prep_w_kernel(w_ref, o_ref, *, tn, n_steps):
    j = pl.program_id(0)
    kinds = [tuple(_col_kind(s * tn + h * LANES) for h in range(tn // LANES)) for s in range(n_steps)]
    for start, stop, ks in _runs(kinds):
        @pl.when((j >= start) & (j < stop))
        def _(ks=ks):
            for h, kind in enumerate(ks):
                sl = slice(h * LANES, (h + 1) * LANES)
                w = w_ref[:, sl]
                if kind == "norm_rope":
                    w = _swap_lane_blocks(w, _QUARTER, 2 * _QUARTER, _QUARTER)
                elif kind == "rope":
                    w = _swap_lane_blocks(w, _HALF_B, LANES // 2, _HALF_B)
                o_ref[:, sl] = w.astype(BF16)


def _prep_w(w, tn=512):
    d, cols = w.shape
    n_steps = cols // tn
    return pl.pallas_call(
        functools.partial(_prep_w_kernel, tn=tn, n_steps=n_steps),
        grid=(n_steps,),
        in_specs=[pl.BlockSpec((d, tn), lambda j: (0, j))],
        out_specs=pl.BlockSpec((d, tn), lambda j: (0, j)),
        out_shape=jax.ShapeDtypeStruct((d, cols), BF16),
        compiler_params=pltpu.CompilerParams(
            dimension_semantics=("parallel",), vmem_limit_bytes=_vmem_limit(2 * d * tn * 6 + 2 * d * tn * 4)),
        name="prep_w",
    )(w)


def _rope_tables(seq):
    pos = jnp.arange(seq)

    def cos_sin(p, dim, theta):
        inv = theta ** (-jnp.arange(0, dim, 2, dtype=F32) / dim)
        ang = p.astype(F32)[:, None] * inv[None, :]
        return jnp.cos(ang), jnp.sin(ang)

    cr, sr = cos_sin(pos // GRID_W, HEAD_DIM // 2, A_ROPE_THETA)
    cc, sc = cos_sin(pos % GRID_W, HEAD_DIM // 2, A_ROPE_THETA)
    cl, sl = cos_sin(pos, PARTIAL_ROPE_DIMS, PARTIAL_ROPE_THETA)
    one = jnp.ones((seq, LANES // 2 - _HALF_B), F32)
    zero = jnp.zeros((seq, LANES // 2 - _HALF_B), F32)
    cos_a = jnp.concatenate([cr, cc, cr, cc], axis=-1)
    sin_a = jnp.concatenate([-sr, -sc, sr, sc], axis=-1)
    cos_b = jnp.concatenate([cl, one, cl, one], axis=-1)
    sin_b = jnp.concatenate([-sl, zero, sl, zero], axis=-1)
    return cos_a, sin_a, cos_b, sin_b


def _inproj_kernel(x_ref, w_ref, ca_ref, sa_ref, cb_ref, sb_ref, cs_ref, bias_ref,
                   qkv_ref, g_ref, xb_ref, *, tn, tc, n_steps):
    j = pl.program_id(1)

    @pl.when(j == 0)
    def _():
        xb_ref[...] = x_ref[...].astype(BF16)

    def epilogue(c, kind, y):
        cols = slice(c * tc, (c + 1) * tc)
        if kind == "gate":
            g_ref[:, cols] = (0.5 * jnp.tanh(0.5 * (y + bias_ref[:, cols])) + 0.5).astype(g_ref.dtype)
        elif kind == "plain":
            qkv_ref[:, cols] = y.astype(BF16)
        else:
            if kind == "norm_rope":
                head = lambda ax: lax.broadcasted_iota(jnp.int32, (tc, tc), ax) // HEAD_DIM
                blockdiag = (head(0) == head(1)).astype(BF16)
                ss = jnp.dot((y * y).astype(BF16), blockdiag, preferred_element_type=F32)
                z = y * lax.rsqrt(ss * (1.0 / HEAD_DIM) + RMS_EPS) * cs_ref[:, cols]
                cos_ref, sin_ref = ca_ref, sa_ref
            else:
                z = y * cs_ref[:, cols]
                cos_ref, sin_ref = cb_ref, sb_ref
            for h in range(tc // LANES):
                zh = z[:, h * LANES:(h + 1) * LANES]
                out = zh * cos_ref[...] + pltpu.roll(zh, LANES // 2, 1) * sin_ref[...]
                qkv_ref[:, c * tc + h * LANES:c * tc + (h + 1) * LANES] = out.astype(BF16)

    kinds = [tuple(_col_kind(s * tn + c * tc) for c in range(tn // tc)) for s in range(n_steps)]
    for start, stop, ks in _runs(kinds):
        @pl.when((j >= start) & (j < stop))
        def _(ks=ks):
            ys = [jnp.dot(xb_ref[...], w_ref[:, c * tc:(c + 1) * tc], preferred_element_type=F32)
                  for c in range(len(ks))]
            for c, kind in enumerate(ks):
                epilogue(c, kind, ys[c])


def _inproj(x2d, w_bf, tabs, colscale, colbias, seq, tm=1024, tn=512, tc=256):
    t, d = x2d.shape
    cols = w_bf.shape[1]
    tm = min(tm, seq)
    n_qkv = QKV_COLS // tn
    n_g = (cols - QKV_COLS) // tn
    per_seq = seq // tm
    tab_spec = pl.BlockSpec((tm, LANES), lambda i, j: (i % per_seq, 0))
    est = (2 * tm * d * 4 + tm * d * 2 + 2 * d * tn * 2 + 8 * tm * LANES * 4 + 2 * tm * tn * 4
           + 4 * tm * tn * 4)
    return pl.pallas_call(
        functools.partial(_inproj_kernel, tn=tn, tc=tc, n_steps=n_qkv + n_g),
        grid=(t // tm, n_qkv + n_g),
        in_specs=[
            pl.BlockSpec((tm, d), lambda i, j: (i, 0)),
            pl.BlockSpec((d, tn), lambda i, j: (0, j)),
            tab_spec, tab_spec, tab_spec, tab_spec,
            pl.BlockSpec((1, tn), lambda i, j: (0, j)),
            pl.BlockSpec((1, tn), lambda i, j: (0, j)),
        ],
        out_specs=[
            pl.BlockSpec((tm, tn), lambda i, j: (i, jnp.minimum(j, n_qkv - 1))),
            pl.BlockSpec((tm, tn), lambda i, j: (i, jnp.clip(j - n_qkv, 0, n_g - 1))),
        ],
        out_shape=[
            jax.ShapeDtypeStruct((t, QKV_COLS), BF16),
            jax.ShapeDtypeStruct((t, cols - QKV_COLS), BF16),
        ],
        scratch_shapes=[pltpu.VMEM((tm, d), BF16)],
        compiler_params=pltpu.CompilerParams(
            dimension_semantics=("arbitrary", "arbitrary"), vmem_limit_bytes=_vmem_limit(est)),
        name="inproj",
    )(x2d, w_bf, *tabs, colscale, colbias)


def _softmax_parts(q, k):
    s = lax.dot_general(q, k, (((1,), (1,)), ((), ())), preferred_element_type=F32)
    p = jnp.exp2(s - jnp.max(s, axis=-1, keepdims=True))
    return p, jnp.sum(p, axis=-1, keepdims=True)


def _attn_a_kernel(q_ref, k_ref, v_ref, o_ref, *, tq, unroll):
    nq = q_ref.shape[0] // tq

    def body(qi, carry):
        r = pl.multiple_of(qi * tq, tq)
        for g in range(A_GROUP):
            sl = slice(g * HEAD_DIM, (g + 1) * HEAD_DIM)
            p, l = _softmax_parts(q_ref[pl.ds(r, tq), sl], k_ref[...])
            o = jnp.dot(p.astype(BF16), v_ref[...], preferred_element_type=F32)
            o_ref[pl.ds(r, tq), sl] = (o / l).astype(BF16)
        return carry

    lax.fori_loop(0, nq, body, 0, unroll=unroll)


def _attn_a(qkv, bsz, seq, tq=256, unroll=4):
    t = qkv.shape[0]
    gw = A_GROUP * HEAD_DIM
    tq = min(tq, seq)
    unroll = min(unroll, seq // tq)
    est = 2 * (2 * seq * gw * 2 + 2 * seq * HEAD_DIM * 2) + 4 * unroll * tq * seq * 4
    return pl.pallas_call(
        functools.partial(_attn_a_kernel, tq=tq, unroll=unroll),
        grid=(bsz, A_KV_HEADS),
        in_specs=[
            pl.BlockSpec((seq, gw), lambda b, h: (b, h)),
            pl.BlockSpec((seq, HEAD_DIM), lambda b, h: (b, OFF_A_K // HEAD_DIM + h)),
            pl.BlockSpec((seq, HEAD_DIM), lambda b, h: (b, OFF_A_V // HEAD_DIM + h)),
        ],
        out_specs=pl.BlockSpec((seq, gw), lambda b, h: (b, h)),
        out_shape=jax.ShapeDtypeStruct((t, A_WIDTH), BF16),
        compiler_params=pltpu.CompilerParams(
            dimension_semantics=("parallel", "parallel"), vmem_limit_bytes=_vmem_limit(est)),
        name="attn_a",
    )(qkv, qkv, qkv)


def _attn_b_kernel(lam_ref, sub_ref, q_ref, k_ref, v_ref, o_ref, *, tq, lam_init, unroll):
    nq = q_ref.shape[0] // tq
    lp = lam_ref[...]
    lam = (jnp.exp(jnp.sum(lp[0:1] * lp[1:2], axis=-1, keepdims=True))
           - jnp.exp(jnp.sum(lp[2:3] * lp[3:4], axis=-1, keepdims=True)) + lam_init)
    d = B_QK_DIM

    def body(qi, carry):
        r = pl.multiple_of(qi * tq, tq)
        p1, l1 = _softmax_parts(q_ref[pl.ds(r, tq), 0:d], k_ref[:, 0:d])
        p2, l2 = _softmax_parts(q_ref[pl.ds(r, tq), d:2 * d], k_ref[:, d:2 * d])
        o1 = jnp.dot(p1.astype(BF16), v_ref[...], preferred_element_type=F32)
        o2 = jnp.dot(p2.astype(BF16), v_ref[...], preferred_element_type=F32)
        o = o1 * (1.0 / l1) - o2 * (lam / l2)
        ms = jnp.mean(o * o, axis=-1, keepdims=True)
        o = o * lax.rsqrt(ms + RMS_EPS) * sub_ref[...] * (1.0 - lam_init)
        o_ref[pl.ds(r, tq), :] = o.astype(BF16)
        return carry

    lax.fori_loop(0, nq, body, 0, unroll=unroll)


def _attn_b(qkv, lam_p, subln, bsz, seq, lam_init, tq=256, unroll=8):
    t = qkv.shape[0]
    tq = min(tq, seq)
    unroll = min(unroll, seq // tq)
    w = 2 * B_QK_DIM
    est = 2 * 4 * seq * w * 2 + 5 * unroll * tq * seq * 4
    return pl.pallas_call(
        functools.partial(_attn_b_kernel, tq=tq, lam_init=lam_init, unroll=unroll),
        grid=(bsz, B_HEADS),
        in_specs=[
            pl.BlockSpec((4, B_QK_DIM), lambda b, h: (0, 0)),
            pl.BlockSpec((1, B_V_DIM), lambda b, h: (0, 0)),
            pl.BlockSpec((seq, w), lambda b, h: (b, OFF_B_Q // w + h)),
            pl.BlockSpec((seq, w), lambda b, h: (b, OFF_B_K // w + h)),
            pl.BlockSpec((seq, B_V_DIM), lambda b, h: (b, OFF_B_V // B_V_DIM + h)),
        ],
        out_specs=pl.BlockSpec((seq, B_V_DIM), lambda b, h: (b, h)),
        out_shape=jax.ShapeDtypeStruct((t, B_WIDTH), BF16),
        compiler_params=pltpu.CompilerParams(
            dimension_semantics=("parallel", "parallel"), vmem_limit_bytes=_vmem_limit(est)),
        name="attn_b",
    )(lam_p, subln, qkv, qkv, qkv)


def _layer_norm(v, g, b):
    mu = jnp.mean(v, axis=-1, keepdims=True)
    c = v - mu
    var = jnp.mean(c * c, axis=-1, keepdims=True)
    return c * lax.rsqrt(var + LN_EPS) * g + b


def _merge_kernel(oa_ref, ob_ref, g0_ref, g1_ref, wa_ref, wb_ref, m_ref):
    ya = jnp.dot(oa_ref[...], wa_ref[...], preferred_element_type=F32)
    yb = jnp.dot(ob_ref[...], wb_ref[...], preferred_element_type=F32)
    m_ref[...] = (g0_ref[...].astype(F32) * ya + g1_ref[...].astype(F32) * yb).astype(BF16)


def _merge(oa, ob, g, wa, wb, tm=1024, tn=1024):
    t = oa.shape[0]
    d = wa.shape[1]
    tm = min(tm, t)
    nn = d // tn
    est = (2 * 2 * tm * A_WIDTH * 2 + 2 * 2 * A_WIDTH * tn * 2 + 2 * 2 * tm * tn * 2 + 2 * tm * tn * 2
           + 4 * tm * tn * 4)
    return pl.pallas_call(
        _merge_kernel,
        grid=(t // tm, nn),
        in_specs=[
            pl.BlockSpec((tm, A_WIDTH), lambda i, n: (i, 0)),
            pl.BlockSpec((tm, B_WIDTH), lambda i, n: (i, 0)),
            pl.BlockSpec((tm, tn), lambda i, n: (i, n)),
            pl.BlockSpec((tm, tn), lambda i, n: (i, nn + n)),
            pl.BlockSpec((A_WIDTH, tn), lambda i, n: (0, n)),
            pl.BlockSpec((B_WIDTH, tn), lambda i, n: (0, n)),
        ],
        out_specs=pl.BlockSpec((tm, tn), lambda i, n: (i, n)),
        out_shape=jax.ShapeDtypeStruct((t, d), BF16),
        compiler_params=pltpu.CompilerParams(
            dimension_semantics=("parallel", "arbitrary"), vmem_limit_bytes=_vmem_limit(est)),
        name="merge",
    )(oa, ob, g, g, wa, wb)


def _outproj_kernel(m_ref, x_ref, wo_ref, lg_ref, lb_ref, wr_ref, x1_ref, x1b_ref, aff_ref, *, alpha):
    ne = aff_ref.shape[1]
    mix = jnp.dot(m_ref[...], wo_ref[...], preferred_element_type=F32)
    x1 = _layer_norm(alpha * x_ref[...] + mix, lg_ref[...], lb_ref[...])
    x1_ref[...] = x1
    hi = x1.astype(BF16)
    x1b_ref[...] = hi
    lo = (x1 - hi.astype(F32)).astype(BF16)
    both = jnp.dot(hi, wr_ref[...], preferred_element_type=F32)
    low = jnp.dot(lo, wr_ref[:, 0:ne], preferred_element_type=F32)
    logits = both[:, 0:ne] + (both[:, ne:2 * ne] + low)
    e = jnp.exp(logits - jnp.max(logits, axis=-1, keepdims=True))
    aff_ref[...] = e / jnp.sum(e, axis=-1, keepdims=True)


def _outproj(merged, x2d, wo, ln_g, ln_b, wr2, alpha, tm=512):
    t, d = x2d.shape
    ne = wr2.shape[1] // 2
    tm = min(tm, t)
    const = lambda shape: pl.BlockSpec(shape, lambda i: (0, 0), pipeline_mode=pl.Buffered(1))
    row = lambda width: pl.BlockSpec((tm, width), lambda i: (i, 0))
    est = d * d * 2 + 2 * tm * d * 2 + 2 * tm * d * 4 + 2 * tm * d * 6 + 5 * tm * d * 4 + d * LANES * 2
    return pl.pallas_call(
        functools.partial(_outproj_kernel, alpha=alpha),
        grid=(t // tm,),
        in_specs=[row(d), row(d), const((d, d)), const((1, d)), const((1, d)), const((d, 2 * ne))],
        out_specs=[row(d), row(d), row(ne)],
        out_shape=[
            jax.ShapeDtypeStruct((t, d), F32),
            jax.ShapeDtypeStruct((t, d), BF16),
            jax.ShapeDtypeStruct((t, ne), F32),
        ],
        compiler_params=pltpu.CompilerParams(
            dimension_semantics=("parallel",), vmem_limit_bytes=_vmem_limit(est)),
        name="outproj",
    )(merged, x2d, wo, ln_g, ln_b, wr2)


def _exclusive_prefix_chunks(mask_chunks, tri):
    out = []
    offset = None
    for m in mask_chunks:
        mf = m.astype(F32)
        incl = jnp.dot(m.astype(BF16), tri, preferred_element_type=F32)
        excl = incl - mf
        out.append(excl if offset is None else excl + offset)
        total = incl[:, LANES - 1:LANES]
        offset = total if offset is None else offset + total
    return out


def _topk_kernel(aff_ref, pos_ref, *, cap):
    ne, seq = aff_ref.shape
    aff = aff_ref[...]

    def search(i, thr):
        cand = thr | lax.shift_left(jnp.int32(1), 30 - i)
        cnt = jnp.sum((aff >= pltpu.bitcast(cand, F32)).astype(jnp.int32), axis=-1, keepdims=True)
        return jnp.where(cnt >= cap, cand, thr)

    thr = lax.fori_loop(0, 31, search, jnp.zeros((ne, 1), jnp.int32))
    gt = aff >= pltpu.bitcast(thr + 1, F32)
    eq = (aff >= pltpu.bitcast(thr, F32)) & jnp.logical_not(gt)
    need = cap - jnp.sum(gt.astype(jnp.int32), axis=-1, keepdims=True)
    tri = (lax.broadcasted_iota(jnp.int32, (LANES, LANES), 0)
           <= lax.broadcasted_iota(jnp.int32, (LANES, LANES), 1)).astype(BF16)
    chunks = [slice(c * LANES, (c + 1) * LANES) for c in range(seq // LANES)]
    eq_rank = _exclusive_prefix_chunks([eq[:, c] for c in chunks], tri)
    sel = [gt[:, c] | (eq[:, c] & (r < need.astype(F32))) for c, r in zip(chunks, eq_rank)]
    slot = _exclusive_prefix_chunks(sel, tri)
    for c, s, p in zip(chunks, sel, slot):
        pos_ref[:, c] = jnp.where(s, p.astype(jnp.int32), -1)


def _topk(aff_t, cap):
    bsz, ne, seq = aff_t.shape
    return pl.pallas_call(
        functools.partial(_topk_kernel, cap=cap),
        grid=(bsz,),
        in_specs=[pl.BlockSpec((None, ne, seq), lambda b: (b, 0, 0))],
        out_specs=pl.BlockSpec((None, ne, seq), lambda b: (b, 0, 0)),
        out_shape=jax.ShapeDtypeStruct((bsz, ne, seq), jnp.int32),
        compiler_params=pltpu.CompilerParams(dimension_semantics=("parallel",)),
        name="topk",
    )(aff_t)


def _gather_kernel(pos_ref, aff_ref, x_ref, xg_ref, gs_ref, *, cap):
    seq = x_ref.shape[0]
    slot = lax.broadcasted_iota(jnp.int32, (cap, seq), 0)
    for e in range(pos_ref.shape[0]):
        hit = pos_ref[e] == slot
        xg = jnp.dot(jnp.where(hit, 1.0, 0.0).astype(BF16), x_ref[...], preferred_element_type=F32)
        xg_ref[e] = xg.astype(BF16)
        gs_ref[e] = jnp.sum(jnp.where(hit, aff_ref[e], 0.0), axis=-1, keepdims=True)


def _gather(pos4, aff4, x1b, cap, eg=8):
    bsz, ne, _, seq = pos4.shape
    d = x1b.shape[1]
    eg = min(eg, ne)
    row_spec = pl.BlockSpec((None, eg, 1, seq), lambda b, e: (b, e, 0, 0))
    est = 2 * seq * d * 2 + 2 * eg * cap * d * 2 + 4 * cap * seq * 4 + 2 * cap * d * 4
    return pl.pallas_call(
        functools.partial(_gather_kernel, cap=cap),
        grid=(bsz, ne // eg),
        in_specs=[row_spec, row_spec, pl.BlockSpec((seq, d), lambda b, e: (b, 0))],
        out_specs=[
            pl.BlockSpec((eg, cap, d), lambda b, e: (e, b, 0)),
            pl.BlockSpec((eg, cap, 1), lambda b, e: (e, b, 0)),
        ],
        out_shape=[
            jax.ShapeDtypeStruct((ne, bsz * cap, d), BF16),
            jax.ShapeDtypeStruct((ne, bsz * cap, 1), F32),
        ],
        compiler_params=pltpu.CompilerParams(
            dimension_semantics=("parallel", "arbitrary"), vmem_limit_bytes=_vmem_limit(est)),
        name="gather",
    )(pos4, aff4, x1b)


def _experts_kernel(xg_ref, gs_ref, wg_ref, wu_ref, wd_ref, og_ref, acc_ref):
    f = pl.program_id(2)

    @pl.when(f == 0)
    def _():
        acc_ref[...] = jnp.zeros_like(acc_ref)

    x = xg_ref[...]
    hg = jnp.dot(x, wg_ref[...].astype(BF16), preferred_element_type=F32)
    hu = jnp.dot(x, wu_ref[...].astype(BF16), preferred_element_type=F32)
    hid = hg * (1.0 / (1.0 + jnp.exp(-hg))) * hu
    acc_ref[...] += jnp.dot(hid.astype(BF16), wd_ref[...].astype(BF16), preferred_element_type=F32)

    @pl.when(f == pl.num_programs(2) - 1)
    def _():
        og_ref[...] = (acc_ref[...] * gs_ref[...]).astype(BF16)


def _experts(xg, gs, w_gate, w_up, w_down, tmx=1024, tf=512):
    ne, rows, d = xg.shape
    ff = w_gate.shape[2]
    tmx = min(tmx, rows)
    tf = min(tf, ff)
    est = (2 * tmx * d * 2 + 2 * tmx * LANES * 4 + 2 * 3 * d * tf * 4 + 2 * tmx * d * 2 + tmx * d * 4
           + 3 * d * tf * 2 + 4 * tmx * tf * 4 + tmx * d * 4)
    return pl.pallas_call(
        _experts_kernel,
        grid=(ne, rows // tmx, ff // tf),
        in_specs=[
            pl.BlockSpec((None, tmx, d), lambda e, m, f: (e, m, 0)),
            pl.BlockSpec((None, tmx, 1), lambda e, m, f: (e, m, 0)),
            pl.BlockSpec((None, d, tf), lambda e, m, f: (e, 0, f)),
            pl.BlockSpec((None, d, tf), lambda e, m, f: (e, 0, f)),
            pl.BlockSpec((None, tf, d), lambda e, m, f: (e, f, 0)),
        ],
        out_specs=pl.BlockSpec((None, tmx, d), lambda e, m, f: (e, m, 0)),
        out_shape=jax.ShapeDtypeStruct((ne, rows, d), BF16),
        scratch_shapes=[pltpu.VMEM((tmx, d), F32)],
        compiler_params=pltpu.CompilerParams(
            dimension_semantics=("parallel", "parallel", "arbitrary"), vmem_limit_bytes=_vmem_limit(est)),
        name="experts",
    )(xg, gs, w_gate, w_up, w_down)


def _combine_kernel(pos_ref, og_ref, x1_ref, lg_ref, lb_ref, o_ref, *, cap, alpha):
    ts = x1_ref.shape[0]
    slot = lax.broadcasted_iota(jnp.int32, (cap, ts), 0)
    y = None
    for e in range(og_ref.shape[0]):
        onehot = jnp.where(pos_ref[e] == slot, 1.0, 0.0).astype(BF16)
        part = lax.dot_general(onehot, og_ref[e], (((0,), (0,)), ((), ())), preferred_element_type=F32)
        y = part if y is None else y + part
    o_ref[...] = _layer_norm(alpha * x1_ref[...] + y, lg_ref[...], lb_ref[...])


def _combine(pos4, og, x1, ln_g, ln_b, cap, alpha, ts=256):
    bsz, ne, _, seq = pos4.shape
    t, d = x1.shape
    ts = min(ts, seq)
    nth = seq // ts
    est = 2 * ne * cap * d * 2 + 4 * ts * d * 4 + 6 * ts * d * 4 + 4 * cap * ts * 4
    return pl.pallas_call(
        functools.partial(_combine_kernel, cap=cap, alpha=alpha),
        grid=(bsz, nth),
        in_specs=[
            pl.BlockSpec((None, ne, 1, ts), lambda b, h: (b, 0, 0, h)),
            pl.BlockSpec((ne, cap, d), lambda b, h: (0, b, 0)),
            pl.BlockSpec((ts, d), lambda b, h: (b * nth + h, 0)),
            pl.BlockSpec((1, d), lambda b, h: (0, 0)),
            pl.BlockSpec((1, d), lambda b, h: (0, 0)),
        ],
        out_specs=pl.BlockSpec((ts, d), lambda b, h: (b * nth + h, 0)),
        out_shape=jax.ShapeDtypeStruct((t, d), F32),
        compiler_params=pltpu.CompilerParams(
            dimension_semantics=("parallel", "arbitrary"), vmem_limit_bytes=_vmem_limit(est)),
        name="combine",
    )(pos4, og, x1, ln_g, ln_b)


def kernel(x, w_in, b_gate, a_q_norm, a_k_norm, b_lambda, b_subln, w_a_proj, w_b_proj, w_o, ln1_g, ln1_b,
           w_router, w_gate, w_up, w_down, ln2_g, ln2_b):
    bsz, seq, d = x.shape
    depth = w_in.shape[0]
    ne = w_router.shape[2]
    cap = CAPACITY_FACTOR * seq // ne
    alpha = (2.0 * depth) ** 0.25
    qscale = HEAD_DIM ** -0.5 * LOG2E
    tabs = _rope_tables(seq)
    ones = lambda n: jnp.ones((n,), F32)

    x2d = x.reshape(bsz * seq, d)
    for l in range(depth):
        lam_init = 0.8 - 0.6 * math.exp(-0.3 * l)
        colscale = jnp.concatenate([
            jnp.tile(a_q_norm[l][_PERM_A], A_Q_HEADS) * qscale, jnp.tile(a_k_norm[l][_PERM_A], A_KV_HEADS),
            ones(COL_A_V), ones(COL_B_Q) * qscale, ones(COL_B_K + COL_B_V + N_BRANCHES * d)])[None, :]
        colbias = jnp.concatenate([jnp.zeros((QKV_COLS,), F32), b_gate[l]])[None, :]
        qkv, gates = _inproj(x2d, _prep_w(w_in[l]), tabs, colscale, colbias, seq)
        oa = _attn_a(qkv, bsz, seq)
        ob = _attn_b(qkv, b_lambda[l], b_subln[l][None, :], bsz, seq, lam_init)
        merged = _merge(oa, ob, gates, w_a_proj[l].astype(BF16), w_b_proj[l].astype(BF16))
        wr_hi = w_router[l].astype(BF16)
        wr_lo = (w_router[l] - wr_hi.astype(F32)).astype(BF16)
        x1, x1b, aff = _outproj(merged, x2d, w_o[l].astype(BF16), ln1_g[l][None, :], ln1_b[l][None, :],
                                jnp.concatenate([wr_hi, wr_lo], axis=1), alpha)
        aff_t = jnp.swapaxes(aff.reshape(bsz, seq, ne), 1, 2)
        pos = _topk(aff_t, cap)
        pos4 = pos.reshape(bsz, ne, 1, seq)
        xg, gs = _gather(pos4, aff_t.reshape(bsz, ne, 1, seq), x1b, cap)
        og = _experts(xg, gs, w_gate[l], w_up[l], w_down[l])
        x2d = _combine(pos4, og, x1, ln2_g[l][None, :], ln2_b[l][None, :], cap, alpha)
    return x2d.reshape(bsz, seq, d)
```

```python
import functools
import math

import numpy as np

import jax
import jax.numpy as jnp
from jax import lax
from jax.experimental import pallas as pl
from jax.experimental.pallas import tpu as pltpu

F32 = jnp.float32
BF16 = jnp.bfloat16

HEAD_DIM = 128
A_Q_HEADS = 8
A_KV_HEADS = 2
A_GROUP = A_Q_HEADS // A_KV_HEADS
A_ROPE_THETA = 10000.0
A_WIDTH = A_Q_HEADS * HEAD_DIM
B_HEADS = 4
B_QK_DIM = 128
B_V_DIM = 2 * B_QK_DIM
B_WIDTH = B_HEADS * B_V_DIM
PARTIAL_ROPE_THETA = 500000.0
PARTIAL_ROPE_DIMS = B_QK_DIM // 4
GRID_W = 64
CAPACITY_FACTOR = 2
RMS_EPS = 1e-6
LN_EPS = 1e-5
N_BRANCHES = 2

COL_A_Q = A_Q_HEADS * HEAD_DIM
COL_A_K = A_KV_HEADS * HEAD_DIM
COL_A_V = A_KV_HEADS * HEAD_DIM
COL_B_Q = B_HEADS * 2 * B_QK_DIM
COL_B_K = B_HEADS * 2 * B_QK_DIM
COL_B_V = B_HEADS * B_V_DIM
QKV_COLS = COL_A_Q + COL_A_K + COL_A_V + COL_B_Q + COL_B_K + COL_B_V
OFF_A_K = COL_A_Q
OFF_A_V = OFF_A_K + COL_A_K
OFF_B_Q = OFF_A_V + COL_A_V
OFF_B_K = OFF_B_Q + COL_B_Q
OFF_B_V = OFF_B_K + COL_B_K

LANES = 128
V7X_VMEM_BYTES = 64 * 1024 * 1024
LOG2E = math.log2(math.e)


def _vmem_limit(estimate_bytes):
    return int(min(estimate_bytes * 5 // 4 + (2 << 20), V7X_VMEM_BYTES - (6 << 20)))


_QUARTER = HEAD_DIM // 4
_PERM_A = np.concatenate([np.arange(0, _QUARTER), np.arange(2 * _QUARTER, 3 * _QUARTER),
                          np.arange(_QUARTER, 2 * _QUARTER), np.arange(3 * _QUARTER, HEAD_DIM)])
_HALF_B = PARTIAL_ROPE_DIMS // 2
_PERM_B = np.arange(B_QK_DIM)
_PERM_B[_HALF_B:2 * _HALF_B] = np.arange(LANES // 2, LANES // 2 + _HALF_B)
_PERM_B[LANES // 2:LANES // 2 + _HALF_B] = np.arange(_HALF_B, 2 * _HALF_B)


def _col_kind(col):
    if col < OFF_A_V:
        return "norm_rope"
    if col < OFF_B_Q:
        return "plain"
    if col < OFF_B_V:
        return "rope"
    if col < QKV_COLS:
        return "plain"
    return "gate"


def _runs(kinds):
    out, start = [], 0
    for i in range(1, len(kinds) + 1):
        if i == len(kinds) or kinds[i] != kinds[start]:
            out.append((start, i, kinds[start]))
            start = i
    return out


def _swap_lane_blocks(w, lo, hi, width):
    lane = lax.broadcasted_iota(jnp.int32, w.shape, 1)
    up = pltpu.roll(w, LANES - (hi - lo), 1)
    down = pltpu.roll(w, hi - lo, 1)
    w = jnp.where((lane >= lo) & (lane < lo + width), up, w)
    return jnp.where((lane >= hi) & (lane < hi + width), down, w)


def _prep_w_kernel(w_ref, oq_ref, og_ref, *, tn, n_qkv, n_steps):
    j = pl.program_id(0)
    kinds = [tuple(_col_kind(s * tn + h * LANES) for h in range(tn // LANES)) for s in range(n_steps)]
    for start, stop, ks in _runs(kinds):
        @pl.when((j >= start) & (j < stop))
        def _(ks=ks, o_ref=oq_ref if start < n_qkv else og_ref):
            for h, kind in enumerate(ks):
                sl = slice(h * LANES, (h + 1) * LANES)
                w = w_ref[:, sl]
                if kind == "norm_rope":
                    w = _swap_lane_blocks(w, _QUARTER, 2 * _QUARTER, _QUARTER)
                elif kind == "rope":
                    w = _swap_lane_blocks(w, _HALF_B, LANES // 2, _HALF_B)
                o_ref[:, sl] = w.astype(BF16)


def _prep_w(w, tn=512):
    d, cols = w.shape
    n_qkv = QKV_COLS // tn
    n_g = (cols - QKV_COLS) // tn
    return pl.pallas_call(
        functools.partial(_prep_w_kernel, tn=tn, n_qkv=n_qkv, n_steps=n_qkv + n_g),
        grid=(n_qkv + n_g,),
        in_specs=[pl.BlockSpec((d, tn), lambda j: (0, j))],
        out_specs=[
            pl.BlockSpec((d, tn), lambda j: (0, jnp.minimum(j, n_qkv - 1))),
            pl.BlockSpec((d, tn), lambda j: (0, jnp.clip(j - n_qkv, 0, n_g - 1))),
        ],
        out_shape=[jax.ShapeDtypeStruct((d, QKV_COLS), BF16), jax.ShapeDtypeStruct((d, cols - QKV_COLS), BF16)],
        compiler_params=pltpu.CompilerParams(
            dimension_semantics=("arbitrary",), vmem_limit_bytes=_vmem_limit(2 * d * tn * 8 + 2 * d * tn * 4)),
        name="prep_w",
    )(w)


def _rope_tables(seq):
    pos = jnp.arange(seq)

    def cos_sin(p, dim, theta):
        inv = theta ** (-jnp.arange(0, dim, 2, dtype=F32) / dim)
        ang = p.astype(F32)[:, None] * inv[None, :]
        return jnp.cos(ang), jnp.sin(ang)

    cr, sr = cos_sin(pos // GRID_W, HEAD_DIM // 2, A_ROPE_THETA)
    cc, sc = cos_sin(pos % GRID_W, HEAD_DIM // 2, A_ROPE_THETA)
    cl, sl = cos_sin(pos, PARTIAL_ROPE_DIMS, PARTIAL_ROPE_THETA)
    one = jnp.ones((seq, LANES // 2 - _HALF_B), F32)
    zero = jnp.zeros((seq, LANES // 2 - _HALF_B), F32)
    cos_a = jnp.concatenate([cr, cc, cr, cc], axis=-1)
    sin_a = jnp.concatenate([-sr, -sc, sr, sc], axis=-1)
    cos_b = jnp.concatenate([cl, one, cl, one], axis=-1)
    sin_b = jnp.concatenate([-sl, zero, sl, zero], axis=-1)
    return cos_a, sin_a, cos_b, sin_b


def _inproj_qkv_kernel(x_ref, w_ref, ca_ref, sa_ref, cb_ref, sb_ref, cs_ref, qkv_ref, xb_ref, *, tn, tc, n_steps):
    j = pl.program_id(1)

    @pl.when(j == 0)
    def _():
        xb_ref[...] = x_ref[...].astype(BF16)

    def epilogue(c, kind, y):
        cols = slice(c * tc, (c + 1) * tc)
        if kind == "plain":
            qkv_ref[:, cols] = y.astype(BF16)
            return
        if kind == "norm_rope":
            head = lambda ax: lax.broadcasted_iota(jnp.int32, (tc, tc), ax) // HEAD_DIM
            blockdiag = (head(0) == head(1)).astype(BF16)
            ss = jnp.dot((y * y).astype(BF16), blockdiag, preferred_element_type=F32)
            z = y * lax.rsqrt(ss * (1.0 / HEAD_DIM) + RMS_EPS) * cs_ref[:, cols]
            cos_ref, sin_ref = ca_ref, sa_ref
        else:
            z = y * cs_ref[:, cols]
            cos_ref, sin_ref = cb_ref, sb_ref
        for h in range(tc // LANES):
            zh = z[:, h * LANES:(h + 1) * LANES]
            out = zh * cos_ref[...] + pltpu.roll(zh, LANES // 2, 1) * sin_ref[...]
            qkv_ref[:, c * tc + h * LANES:c * tc + (h + 1) * LANES] = out.astype(BF16)

    kinds = [tuple(_col_kind(s * tn + c * tc) for c in range(tn // tc)) for s in range(n_steps)]
    for start, stop, ks in _runs(kinds):
        @pl.when((j >= start) & (j < stop))
        def _(ks=ks):
            ys = [jnp.dot(xb_ref[...], w_ref[:, c * tc:(c + 1) * tc], preferred_element_type=F32)
                  for c in range(len(ks))]
            for c, kind in enumerate(ks):
                epilogue(c, kind, ys[c])


def _inproj_qkv(x2d, w_qkv, tabs, colscale, seq, tm=1024, tn=768, tc=256):
    t, d = x2d.shape
    tm = min(tm, seq)
    n_steps = QKV_COLS // tn
    per_seq = seq // tm
    tab_spec = pl.BlockSpec((tm, LANES), lambda i, j: (i % per_seq, 0))
    est = (2 * tm * d * 4 + 2 * tm * d * 2 + 2 * d * tn * 2 + 8 * tm * LANES * 4 + 2 * tm * tn * 2
           + 3 * tm * tn * 4)
    return pl.pallas_call(
        functools.partial(_inproj_qkv_kernel, tn=tn, tc=tc, n_steps=n_steps),
        grid=(t // tm, n_steps),
        in_specs=[
            pl.BlockSpec((tm, d), lambda i, j: (i, 0)),
            pl.BlockSpec((d, tn), lambda i, j: (0, j)),
            tab_spec, tab_spec, tab_spec, tab_spec,
            pl.BlockSpec((1, tn), lambda i, j: (0, j)),
        ],
        out_specs=[
            pl.BlockSpec((tm, tn), lambda i, j: (i, j)),
            pl.BlockSpec((tm, d), lambda i, j: (i, 0)),
        ],
        out_shape=[jax.ShapeDtypeStruct((t, QKV_COLS), BF16), jax.ShapeDtypeStruct((t, d), BF16)],
        compiler_params=pltpu.CompilerParams(
            dimension_semantics=("arbitrary", "arbitrary"), vmem_limit_bytes=_vmem_limit(est)),
        name="inproj_qkv",
    )(x2d, w_qkv, *tabs, colscale)


def _inproj_gates_kernel(x_ref, w_ref, bias_ref, g_ref, *, tc):
    n = w_ref.shape[1] // tc
    ys = [jnp.dot(x_ref[...], w_ref[:, c * tc:(c + 1) * tc], preferred_element_type=F32) for c in range(n)]
    for c in range(n):
        cols = slice(c * tc, (c + 1) * tc)
        g_ref[:, cols] = (0.5 * jnp.tanh(0.5 * (ys[c] + bias_ref[:, cols])) + 0.5).astype(BF16)


def _inproj_gates(xb, w_gate, bias, tm=2048, tn=1024, tc=256):
    t, d = xb.shape
    cols = w_gate.shape[1]
    tm = min(tm, t)
    tn = min(tn, cols)
    est = 2 * tm * d * 2 + 2 * d * tn * 2 + 2 * tm * tn * 2 + 3 * tm * tn * 4
    return pl.pallas_call(
        functools.partial(_inproj_gates_kernel, tc=tc),
        grid=(t // tm, cols // tn),
        in_specs=[
            pl.BlockSpec((tm, d), lambda i, j: (i, 0)),
            pl.BlockSpec((d, tn), lambda i, j: (0, j)),
            pl.BlockSpec((1, tn), lambda i, j: (0, j)),
        ],
        out_specs=pl.BlockSpec((tm, tn), lambda i, j: (i, j)),
        out_shape=jax.ShapeDtypeStruct((t, cols), BF16),
        compiler_params=pltpu.CompilerParams(
            dimension_semantics=("parallel", "arbitrary"), vmem_limit_bytes=_vmem_limit(est)),
        name="inproj_gates",
    )(xb, w_gate, bias)


def _softmax_parts(q, k):
    s = lax.dot_general(q, k, (((1,), (1,)), ((), ())), preferred_element_type=F32)
    p = jnp.exp2(s - jnp.max(s, axis=-1, keepdims=True))
    return p, jnp.sum(p, axis=-1, keepdims=True)


def _attn_a_kernel(q_ref, k_ref, v_ref, o_ref, *, tq, unroll):
    nq = q_ref.shape[0] // tq

    def body(qi, carry):
        r = pl.multiple_of(qi * tq, tq)
        for g in range(A_GROUP):
            sl = slice(g * HEAD_DIM, (g + 1) * HEAD_DIM)
            p, l = _softmax_parts(q_ref[pl.ds(r, tq), sl], k_ref[...])
            o = jnp.dot(p.astype(BF16), v_ref[...], preferred_element_type=F32)
            o_ref[pl.ds(r, tq), sl] = (o / l).astype(BF16)
        return carry

    lax.fori_loop(0, nq, body, 0, unroll=unroll)


def _attn_a(qkv, bsz, seq, tq=256, unroll=4):
    t = qkv.shape[0]
    gw = A_GROUP * HEAD_DIM
    tq = min(tq, seq)
    unroll = min(unroll, seq // tq)
    est = 2 * (2 * seq * gw * 2 + 2 * seq * HEAD_DIM * 2) + 4 * unroll * tq * seq * 4
    return pl.pallas_call(
        functools.partial(_attn_a_kernel, tq=tq, unroll=unroll),
        grid=(bsz, A_KV_HEADS),
        in_specs=[
            pl.BlockSpec((seq, gw), lambda b, h: (b, h)),
            pl.BlockSpec((seq, HEAD_DIM), lambda b, h: (b, OFF_A_K // HEAD_DIM + h)),
            pl.BlockSpec((seq, HEAD_DIM), lambda b, h: (b, OFF_A_V // HEAD_DIM + h)),
        ],
        out_specs=pl.BlockSpec((seq, gw), lambda b, h: (b, h)),
        out_shape=jax.ShapeDtypeStruct((t, A_WIDTH), BF16),
        compiler_params=pltpu.CompilerParams(
            dimension_semantics=("parallel", "parallel"), vmem_limit_bytes=_vmem_limit(est)),
        name="attn_a",
    )(qkv, qkv, qkv)


def _attn_b_kernel(lam_ref, sub_ref, q_ref, k_ref, v_ref, o_ref, *, tq, lam_init, unroll):
    nq = q_ref.shape[0] // tq
    lp = lam_ref[...]
    lam = (jnp.exp(jnp.sum(lp[0:1] * lp[1:2], axis=-1, keepdims=True))
           - jnp.exp(jnp.sum(lp[2:3] * lp[3:4], axis=-1, keepdims=True)) + lam_init)
    d = B_QK_DIM

    def body(qi, carry):
        r = pl.multiple_of(qi * tq, tq)
        p1, l1 = _softmax_parts(q_ref[pl.ds(r, tq), 0:d], k_ref[:, 0:d])
        p2, l2 = _softmax_parts(q_ref[pl.ds(r, tq), d:2 * d], k_ref[:, d:2 * d])
        o1 = jnp.dot(p1.astype(BF16), v_ref[...], preferred_element_type=F32)
        o2 = jnp.dot(p2.astype(BF16), v_ref[...], preferred_element_type=F32)
        o = o1 * (1.0 / l1) - o2 * (lam / l2)
        ms = jnp.mean(o * o, axis=-1, keepdims=True)
        o = o * lax.rsqrt(ms + RMS_EPS) * sub_ref[...] * (1.0 - lam_init)
        o_ref[pl.ds(r, tq), :] = o.astype(BF16)
        return carry

    lax.fori_loop(0, nq, body, 0, unroll=unroll)


def _attn_b(qkv, lam_p, subln, bsz, seq, lam_init, tq=256, unroll=8):
    t = qkv.shape[0]
    tq = min(tq, seq)
    unroll = min(unroll, seq // tq)
    w = 2 * B_QK_DIM
    est = 2 * 4 * seq * w * 2 + 5 * unroll * tq * seq * 4
    return pl.pallas_call(
        functools.partial(_attn_b_kernel, tq=tq, lam_init=lam_init, unroll=unroll),
        grid=(bsz, B_HEADS),
        in_specs=[
            pl.BlockSpec((4, B_QK_DIM), lambda b, h: (0, 0)),
            pl.BlockSpec((1, B_V_DIM), lambda b, h: (0, 0)),
            pl.BlockSpec((seq, w), lambda b, h: (b, OFF_B_Q // w + h)),
            pl.BlockSpec((seq, w), lambda b, h: (b, OFF_B_K // w + h)),
            pl.BlockSpec((seq, B_V_DIM), lambda b, h: (b, OFF_B_V // B_V_DIM + h)),
        ],
        out_specs=pl.BlockSpec((seq, B_V_DIM), lambda b, h: (b, h)),
        out_shape=jax.ShapeDtypeStruct((t, B_WIDTH), BF16),
        compiler_params=pltpu.CompilerParams(
            dimension_semantics=("parallel", "parallel"), vmem_limit_bytes=_vmem_limit(est)),
        name="attn_b",
    )(lam_p, subln, qkv, qkv, qkv)


def _layer_norm(v, g, b):
    mu = jnp.mean(v, axis=-1, keepdims=True)
    c = v - mu
    var = jnp.mean(c * c, axis=-1, keepdims=True)
    return c * lax.rsqrt(var + LN_EPS) * g + b


def _merge_kernel(oa_ref, ob_ref, g0_ref, g1_ref, wa_ref, wb_ref, m_ref):
    ya = jnp.dot(oa_ref[...], wa_ref[...], preferred_element_type=F32)
    yb = jnp.dot(ob_ref[...], wb_ref[...], preferred_element_type=F32)
    m_ref[...] = (g0_ref[...].astype(F32) * ya + g1_ref[...].astype(F32) * yb).astype(BF16)


def _merge(oa, ob, g, wa, wb, tm=1024, tn=1024):
    t = oa.shape[0]
    d = wa.shape[1]
    tm = min(tm, t)
    tn = min(tn, d)
    nn = d // tn
    est = (2 * 2 * tm * A_WIDTH * 2 + 2 * 2 * A_WIDTH * tn * 2 + 2 * 2 * tm * tn * 2 + 2 * tm * tn * 2
           + 4 * tm * tn * 4)
    return pl.pallas_call(
        _merge_kernel,
        grid=(t // tm, nn),
        in_specs=[
            pl.BlockSpec((tm, A_WIDTH), lambda i, n: (i, 0)),
            pl.BlockSpec((tm, B_WIDTH), lambda i, n: (i, 0)),
            pl.BlockSpec((tm, tn), lambda i, n: (i, n)),
            pl.BlockSpec((tm, tn), lambda i, n: (i, nn + n)),
            pl.BlockSpec((A_WIDTH, tn), lambda i, n: (0, n)),
            pl.BlockSpec((B_WIDTH, tn), lambda i, n: (0, n)),
        ],
        out_specs=pl.BlockSpec((tm, tn), lambda i, n: (i, n)),
        out_shape=jax.ShapeDtypeStruct((t, d), BF16),
        compiler_params=pltpu.CompilerParams(
            dimension_semantics=("parallel", "arbitrary"), vmem_limit_bytes=_vmem_limit(est)),
        name="merge",
    )(oa, ob, g, g, wa, wb)


def _outproj_kernel(m_ref, x_ref, wo_ref, lg_ref, lb_ref, wr_ref, x1_ref, x1b_ref, aff_ref, *, alpha):
    ne = aff_ref.shape[1]
    mix = jnp.dot(m_ref[...], wo_ref[...], preferred_element_type=F32)
    x1 = _layer_norm(alpha * x_ref[...] + mix, lg_ref[...], lb_ref[...])
    x1_ref[...] = x1
    hi = x1.astype(BF16)
    x1b_ref[...] = hi
    lo = (x1 - hi.astype(F32)).astype(BF16)
    both = jnp.dot(hi, wr_ref[...], preferred_element_type=F32)
    low = jnp.dot(lo, wr_ref[:, 0:ne], preferred_element_type=F32)
    logits = both[:, 0:ne] + (both[:, ne:2 * ne] + low)
    e = jnp.exp(logits - jnp.max(logits, axis=-1, keepdims=True))
    aff_ref[...] = e / jnp.sum(e, axis=-1, keepdims=True)


def _outproj(merged, x2d, wo, ln_g, ln_b, wr2, alpha, tm=512):
    t, d = x2d.shape
    ne = wr2.shape[1] // 2
    tm = min(tm, t)
    const = lambda shape: pl.BlockSpec(shape, lambda i: (0, 0), pipeline_mode=pl.Buffered(1))
    row = lambda width: pl.BlockSpec((tm, width), lambda i: (i, 0))
    est = d * d * 2 + 2 * tm * d * 2 + 2 * tm * d * 4 + 2 * tm * d * 6 + 5 * tm * d * 4 + d * LANES * 2
    return pl.pallas_call(
        functools.partial(_outproj_kernel, alpha=alpha),
        grid=(t // tm,),
        in_specs=[row(d), row(d), const((d, d)), const((1, d)), const((1, d)), const((d, 2 * ne))],
        out_specs=[row(d), row(d), row(ne)],
        out_shape=[
            jax.ShapeDtypeStruct((t, d), F32),
            jax.ShapeDtypeStruct((t, d), BF16),
            jax.ShapeDtypeStruct((t, ne), F32),
        ],
        compiler_params=pltpu.CompilerParams(
            dimension_semantics=("parallel",), vmem_limit_bytes=_vmem_limit(est)),
        name="outproj",
    )(merged, x2d, wo, ln_g, ln_b, wr2)


def _exclusive_prefix_chunks(mask_chunks, tri):
    out = []
    offset = None
    for m in mask_chunks:
        mf = m.astype(F32)
        incl = jnp.dot(m.astype(BF16), tri, preferred_element_type=F32)
        excl = incl - mf
        out.append(excl if offset is None else excl + offset)
        total = incl[:, LANES - 1:LANES]
        offset = total if offset is None else offset + total
    return out


def _topk_kernel(aff_ref, pos_ref, *, cap):
    ne, seq = aff_ref.shape
    aff = aff_ref[...]

    def search(i, thr):
        cand = thr | lax.shift_left(jnp.int32(1), 30 - i)
        cnt = jnp.sum((aff >= pltpu.bitcast(cand, F32)).astype(jnp.int32), axis=-1, keepdims=True)
        return jnp.where(cnt >= cap, cand, thr)

    thr = lax.fori_loop(0, 31, search, jnp.zeros((ne, 1), jnp.int32))
    gt = aff >= pltpu.bitcast(thr + 1, F32)
    eq = (aff >= pltpu.bitcast(thr, F32)) & jnp.logical_not(gt)
    need = cap - jnp.sum(gt.astype(jnp.int32), axis=-1, keepdims=True)
    tri = (lax.broadcasted_iota(jnp.int32, (LANES, LANES), 0)
           <= lax.broadcasted_iota(jnp.int32, (LANES, LANES), 1)).astype(BF16)
    chunks = [slice(c * LANES, (c + 1) * LANES) for c in range(seq // LANES)]
    eq_rank = _exclusive_prefix_chunks([eq[:, c] for c in chunks], tri)
    sel = [gt[:, c] | (eq[:, c] & (r < need.astype(F32))) for c, r in zip(chunks, eq_rank)]
    slot = _exclusive_prefix_chunks(sel, tri)
    for c, s, p in zip(chunks, sel, slot):
        pos_ref[:, c] = jnp.where(s, p.astype(jnp.int32), -1)


def _topk(aff_t, cap):
    bsz, ne, seq = aff_t.shape
    return pl.pallas_call(
        functools.partial(_topk_kernel, cap=cap),
        grid=(bsz,),
        in_specs=[pl.BlockSpec((None, ne, seq), lambda b: (b, 0, 0))],
        out_specs=pl.BlockSpec((None, ne, seq), lambda b: (b, 0, 0)),
        out_shape=jax.ShapeDtypeStruct((bsz, ne, seq), jnp.int32),
        compiler_params=pltpu.CompilerParams(dimension_semantics=("parallel",)),
        name="topk",
    )(aff_t)


def _gather_kernel(pos_ref, aff_ref, x_ref, xg_ref, gs_ref, *, cap):
    seq = x_ref.shape[0]
    slot = lax.broadcasted_iota(jnp.int32, (cap, seq), 0)
    for e in range(pos_ref.shape[0]):
        hit = pos_ref[e] == slot
        xg = jnp.dot(jnp.where(hit, 1.0, 0.0).astype(BF16), x_ref[...], preferred_element_type=F32)
        xg_ref[e] = xg.astype(BF16)
        gs_ref[e] = jnp.sum(jnp.where(hit, aff_ref[e], 0.0), axis=-1, keepdims=True)


def _gather(pos4, aff4, x1b, cap, eg=8):
    bsz, ne, _, seq = pos4.shape
    d = x1b.shape[1]
    eg = min(eg, ne)
    row_spec = pl.BlockSpec((None, eg, 1, seq), lambda b, e: (b, e, 0, 0))
    est = 2 * seq * d * 2 + 2 * eg * cap * d * 2 + 4 * cap * seq * 4 + 2 * cap * d * 4
    return pl.pallas_call(
        functools.partial(_gather_kernel, cap=cap),
        grid=(bsz, ne // eg),
        in_specs=[row_spec, row_spec, pl.BlockSpec((seq, d), lambda b, e: (b, 0))],
        out_specs=[
            pl.BlockSpec((eg, cap, d), lambda b, e: (e, b, 0)),
            pl.BlockSpec((eg, cap, 1), lambda b, e: (e, b, 0)),
        ],
        out_shape=[
            jax.ShapeDtypeStruct((ne, bsz * cap, d), BF16),
            jax.ShapeDtypeStruct((ne, bsz * cap, 1), F32),
        ],
        compiler_params=pltpu.CompilerParams(
            dimension_semantics=("parallel", "arbitrary"), vmem_limit_bytes=_vmem_limit(est)),
        name="gather",
    )(pos4, aff4, x1b)


def _experts_kernel(xg_ref, gs_ref, wg_ref, wu_ref, wd_ref, og_ref, acc_ref):
    f = pl.program_id(2)

    @pl.when(f == 0)
    def _():
        acc_ref[...] = jnp.zeros_like(acc_ref)

    x = xg_ref[...]
    hg = jnp.dot(x, wg_ref[...].astype(BF16), preferred_element_type=F32)
    hu = jnp.dot(x, wu_ref[...].astype(BF16), preferred_element_type=F32)
    hid = hg * (1.0 / (1.0 + jnp.exp(-hg))) * hu
    acc_ref[...] += jnp.dot(hid.astype(BF16), wd_ref[...].astype(BF16), preferred_element_type=F32)

    @pl.when(f == pl.num_programs(2) - 1)
    def _():
        og_ref[...] = (acc_ref[...] * gs_ref[...]).astype(BF16)


def _experts(xg, gs, w_gate, w_up, w_down, tmx=1024, tf=512):
    ne, rows, d = xg.shape
    ff = w_gate.shape[2]
    tmx = min(tmx, rows)
    tf = min(tf, ff)
    est = (2 * tmx * d * 2 + 2 * tmx * LANES * 4 + 2 * 3 * d * tf * 4 + 2 * tmx * d * 2 + tmx * d * 4
           + 3 * d * tf * 2 + 4 * tmx * tf * 4 + tmx * d * 4)
    return pl.pallas_call(
        _experts_kernel,
        grid=(ne, rows // tmx, ff // tf),
        in_specs=[
            pl.BlockSpec((None, tmx, d), lambda e, m, f: (e, m, 0)),
            pl.BlockSpec((None, tmx, 1), lambda e, m, f: (e, m, 0)),
            pl.BlockSpec((None, d, tf), lambda e, m, f: (e, 0, f)),
            pl.BlockSpec((None, d, tf), lambda e, m, f: (e, 0, f)),
            pl.BlockSpec((None, tf, d), lambda e, m, f: (e, f, 0)),
        ],
        out_specs=pl.BlockSpec((None, tmx, d), lambda e, m, f: (e, m, 0)),
        out_shape=jax.ShapeDtypeStruct((ne, rows, d), BF16),
        scratch_shapes=[pltpu.VMEM((tmx, d), F32)],
        compiler_params=pltpu.CompilerParams(
            dimension_semantics=("parallel", "parallel", "arbitrary"), vmem_limit_bytes=_vmem_limit(est)),
        name="experts",
    )(xg, gs, w_gate, w_up, w_down)


def _combine_kernel(pos_ref, og_ref, x1_ref, lg_ref, lb_ref, o_ref, *, cap, alpha):
    ts = x1_ref.shape[0]
    slot = lax.broadcasted_iota(jnp.int32, (cap, ts), 0)
    y = None
    for e in range(og_ref.shape[0]):
        onehot = jnp.where(pos_ref[e] == slot, 1.0, 0.0).astype(BF16)
        part = lax.dot_general(onehot, og_ref[e], (((0,), (0,)), ((), ())), preferred_element_type=F32)
        y = part if y is None else y + part
    o_ref[...] = _layer_norm(alpha * x1_ref[...] + y, lg_ref[...], lb_ref[...])


def _combine(pos4, og, x1, ln_g, ln_b, cap, alpha, ts=256):
    bsz, ne, _, seq = pos4.shape
    t, d = x1.shape
    ts = min(ts, seq)
    nth = seq // ts
    est = 2 * ne * cap * d * 2 + 4 * ts * d * 4 + 6 * ts * d * 4 + 4 * cap * ts * 4
    return pl.pallas_call(
        functools.partial(_combine_kernel, cap=cap, alpha=alpha),
        grid=(bsz, nth),
        in_specs=[
            pl.BlockSpec((None, ne, 1, ts), lambda b, h: (b, 0, 0, h)),
            pl.BlockSpec((ne, cap, d), lambda b, h: (0, b, 0)),
            pl.BlockSpec((ts, d), lambda b, h: (b * nth + h, 0)),
            pl.BlockSpec((1, d), lambda b, h: (0, 0)),
            pl.BlockSpec((1, d), lambda b, h: (0, 0)),
        ],
        out_specs=pl.BlockSpec((ts, d), lambda b, h: (b * nth + h, 0)),
        out_shape=jax.ShapeDtypeStruct((t, d), F32),
        compiler_params=pltpu.CompilerParams(
            dimension_semantics=("parallel", "arbitrary"), vmem_limit_bytes=_vmem_limit(est)),
        name="combine",
    )(pos4, og, x1, ln_g, ln_b)


def kernel(x, w_in, b_gate, a_q_norm, a_k_norm, b_lambda, b_subln, w_a_proj, w_b_proj, w_o, ln1_g, ln1_b,
           w_router, w_gate, w_up, w_down, ln2_g, ln2_b):
    bsz, seq, d = x.shape
    depth = w_in.shape[0]
    ne = w_router.shape[2]
    cap = CAPACITY_FACTOR * seq // ne
    alpha = (2.0 * depth) ** 0.25
    qscale = HEAD_DIM ** -0.5 * LOG2E
    tabs = _rope_tables(seq)
    ones = lambda n: jnp.ones((n,), F32)

    x2d = x.reshape(bsz * seq, d)
    for l in range(depth):
        lam_init = 0.8 - 0.6 * math.exp(-0.3 * l)
        colscale = jnp.concatenate([
            jnp.tile(a_q_norm[l][_PERM_A], A_Q_HEADS) * qscale, jnp.tile(a_k_norm[l][_PERM_A], A_KV_HEADS),
            ones(COL_A_V), ones(COL_B_Q) * qscale, ones(COL_B_K + COL_B_V)])[None, :]
        w_qkv, w_g = _prep_w(w_in[l])
        qkv, xb = _inproj_qkv(x2d, w_qkv, tabs, colscale, seq)
        gates = _inproj_gates(xb, w_g, b_gate[l][None, :])
        oa = _attn_a(qkv, bsz, seq)
        ob = _attn_b(qkv, b_lambda[l], b_subln[l][None, :], bsz, seq, lam_init)
        merged = _merge(oa, ob, gates, w_a_proj[l].astype(BF16), w_b_proj[l].astype(BF16))
        wr_hi = w_router[l].astype(BF16)
        wr_lo = (w_router[l] - wr_hi.astype(F32)).astype(BF16)
        x1, x1b, aff = _outproj(merged, x2d, w_o[l].astype(BF16), ln1_g[l][None, :], ln1_b[l][None, :],
                                jnp.concatenate([wr_hi, wr_lo], axis=1), alpha)
        aff_t = jnp.swapaxes(aff.reshape(bsz, seq, ne), 1, 2)
        pos = _topk(aff_t, cap)
        pos4 = pos.reshape(bsz, ne, 1, seq)
        xg, gs = _gather(pos4, aff_t.reshape(bsz, ne, 1, seq), x1b, cap)
        og = _experts(xg, gs, w_gate[l], w_up[l], w_down[l])
        x2d = _combine(pos4, og, x1, ln2_g[l][None, :], ln2_b[l][None, :], cap, alpha)
    return x2d.reshape(bsz, seq, d)
```

```python
import functools
import math

import numpy as np

import jax
import jax.numpy as jnp
from jax import lax
from jax.experimental import pallas as pl
from jax.experimental.pallas import tpu as pltpu

F32 = jnp.float32
BF16 = jnp.bfloat16

HEAD_DIM = 128
A_Q_HEADS = 8
A_KV_HEADS = 2
A_GROUP = A_Q_HEADS // A_KV_HEADS
A_ROPE_THETA = 10000.0
A_WIDTH = A_Q_HEADS * HEAD_DIM
B_HEADS = 4
B_QK_DIM = 128
B_V_DIM = 2 * B_QK_DIM
B_WIDTH = B_HEADS * B_V_DIM
PARTIAL_ROPE_THETA = 500000.0
PARTIAL_ROPE_DIMS = B_QK_DIM // 4
GRID_W = 64
CAPACITY_FACTOR = 2
RMS_EPS = 1e-6
LN_EPS = 1e-5

COL_A_Q = A_Q_HEADS * HEAD_DIM
COL_A_K = A_KV_HEADS * HEAD_DIM
COL_A_V = A_KV_HEADS * HEAD_DIM
COL_B_Q = B_HEADS * 2 * B_QK_DIM
COL_B_K = B_HEADS * 2 * B_QK_DIM
COL_B_V = B_HEADS * B_V_DIM
QKV_COLS = COL_A_Q + COL_A_K + COL_A_V + COL_B_Q + COL_B_K + COL_B_V
OFF_A_K = COL_A_Q
OFF_A_V = OFF_A_K + COL_A_K
OFF_B_Q = OFF_A_V + COL_A_V
OFF_B_K = OFF_B_Q + COL_B_Q
OFF_B_V = OFF_B_K + COL_B_K

LANES = 128
V7X_VMEM_BYTES = 64 * 1024 * 1024
LOG2E = math.log2(math.e)


def _vmem_limit(estimate_bytes):
    return int(min(estimate_bytes * 5 // 4 + (2 << 20), V7X_VMEM_BYTES - (6 << 20)))


_QUARTER = HEAD_DIM // 4
_PERM_A = np.concatenate([np.arange(0, _QUARTER), np.arange(2 * _QUARTER, 3 * _QUARTER),
                          np.arange(_QUARTER, 2 * _QUARTER), np.arange(3 * _QUARTER, HEAD_DIM)])
_HALF_B = PARTIAL_ROPE_DIMS // 2
_PERM_B = np.arange(B_QK_DIM)
_PERM_B[_HALF_B:2 * _HALF_B] = np.arange(LANES // 2, LANES // 2 + _HALF_B)
_PERM_B[LANES // 2:LANES // 2 + _HALF_B] = np.arange(_HALF_B, 2 * _HALF_B)


def _col_kind(col):
    if col < OFF_A_V:
        return "norm_rope"
    if col < OFF_B_Q:
        return "plain"
    if col < OFF_B_V:
        return "rope"
    if col < QKV_COLS:
        return "plain"
    return "gate"


def _runs(kinds):
    out, start = [], 0
    for i in range(1, len(kinds) + 1):
        if i == len(kinds) or kinds[i] != kinds[start]:
            out.append((start, i, kinds[start]))
            start = i
    return out


def _swap_lane_blocks(w, lo, hi, width):
    lane = lax.broadcasted_iota(jnp.int32, w.shape, 1)
    up = pltpu.roll(w, LANES - (hi - lo), 1)
    down = pltpu.roll(w, hi - lo, 1)
    w = jnp.where((lane >= lo) & (lane < lo + width), up, w)
    return jnp.where((lane >= hi) & (lane < hi + width), down, w)


def _prep_w_kernel(w_ref, oq_ref, og_ref, *, tn, n_qkv, n_steps):
    j = pl.program_id(0)
    kinds = [tuple(_col_kind(s * tn + h * LANES) for h in range(tn // LANES)) for s in range(n_steps)]
    for start, stop, ks in _runs(kinds):
        @pl.when((j >= start) & (j < stop))
        def _(ks=ks, o_ref=oq_ref if start < n_qkv else og_ref):
            for h, kind in enumerate(ks):
                sl = slice(h * LANES, (h + 1) * LANES)
                w = w_ref[:, sl]
                if kind == "norm_rope":
                    w = _swap_lane_blocks(w, _QUARTER, 2 * _QUARTER, _QUARTER)
                elif kind == "rope":
                    w = _swap_lane_blocks(w, _HALF_B, LANES // 2, _HALF_B)
                o_ref[:, sl] = w.astype(BF16)


def _prep_w(w, tn=512):
    d, cols = w.shape
    n_qkv = QKV_COLS // tn
    n_g = (cols - QKV_COLS) // tn
    return pl.pallas_call(
        functools.partial(_prep_w_kernel, tn=tn, n_qkv=n_qkv, n_steps=n_qkv + n_g),
        grid=(n_qkv + n_g,),
        in_specs=[pl.BlockSpec((d, tn), lambda j: (0, j))],
        out_specs=[
            pl.BlockSpec((d, tn), lambda j: (0, jnp.minimum(j, n_qkv - 1))),
            pl.BlockSpec((d, tn), lambda j: (0, jnp.clip(j - n_qkv, 0, n_g - 1))),
        ],
        out_shape=[jax.ShapeDtypeStruct((d, QKV_COLS), BF16), jax.ShapeDtypeStruct((d, cols - QKV_COLS), BF16)],
        compiler_params=pltpu.CompilerParams(
            dimension_semantics=("arbitrary",), vmem_limit_bytes=_vmem_limit(2 * d * tn * 8 + 2 * d * tn * 4)),
        name="prep_w",
    )(w)


def _rope_tables(seq):
    pos = jnp.arange(seq)

    def cos_sin(p, dim, theta):
        inv = theta ** (-jnp.arange(0, dim, 2, dtype=F32) / dim)
        ang = p.astype(F32)[:, None] * inv[None, :]
        return jnp.cos(ang), jnp.sin(ang)

    cr, sr = cos_sin(pos // GRID_W, HEAD_DIM // 2, A_ROPE_THETA)
    cc, sc = cos_sin(pos % GRID_W, HEAD_DIM // 2, A_ROPE_THETA)
    cl, sl = cos_sin(pos, PARTIAL_ROPE_DIMS, PARTIAL_ROPE_THETA)
    one = jnp.ones((seq, LANES // 2 - _HALF_B), F32)
    zero = jnp.zeros((seq, LANES // 2 - _HALF_B), F32)
    cos_a = jnp.concatenate([cr, cc, cr, cc], axis=-1)
    sin_a = jnp.concatenate([-sr, -sc, sr, sc], axis=-1)
    cos_b = jnp.concatenate([cl, one, cl, one], axis=-1)
    sin_b = jnp.concatenate([-sl, zero, sl, zero], axis=-1)
    return cos_a, sin_a, cos_b, sin_b


def _inproj_qkv_kernel(x_ref, w_ref, ca_ref, sa_ref, cb_ref, sb_ref, cs_ref, qkv_ref, xb_ref, *, tn, tc, n_steps):
    j = pl.program_id(1)

    @pl.when(j == 0)
    def _():
        xb_ref[...] = x_ref[...].astype(BF16)

    def epilogue(c, kind, y):
        cols = slice(c * tc, (c + 1) * tc)
        if kind == "plain":
            qkv_ref[:, cols] = y.astype(BF16)
            return
        if kind == "norm_rope":
            head = lambda ax: lax.broadcasted_iota(jnp.int32, (tc, tc), ax) // HEAD_DIM
            blockdiag = (head(0) == head(1)).astype(BF16)
            ss = jnp.dot((y * y).astype(BF16), blockdiag, preferred_element_type=F32)
            z = y * lax.rsqrt(ss * (1.0 / HEAD_DIM) + RMS_EPS) * cs_ref[:, cols]
            cos_ref, sin_ref = ca_ref, sa_ref
        else:
            z = y * cs_ref[:, cols]
            cos_ref, sin_ref = cb_ref, sb_ref
        for h in range(tc // LANES):
            zh = z[:, h * LANES:(h + 1) * LANES]
            out = zh * cos_ref[...] + pltpu.roll(zh, LANES // 2, 1) * sin_ref[...]
            qkv_ref[:, c * tc + h * LANES:c * tc + (h + 1) * LANES] = out.astype(BF16)

    kinds = [tuple(_col_kind(s * tn + c * tc) for c in range(tn // tc)) for s in range(n_steps)]
    for start, stop, ks in _runs(kinds):
        @pl.when((j >= start) & (j < stop))
        def _(ks=ks):
            ys = [jnp.dot(xb_ref[...], w_ref[:, c * tc:(c + 1) * tc], preferred_element_type=F32)
                  for c in range(len(ks))]
            for c, kind in enumerate(ks):
                epilogue(c, kind, ys[c])


def _inproj_qkv(x2d, w_qkv, tabs, colscale, seq, tm=1024, tn=768, tc=256):
    t, d = x2d.shape
    tm = min(tm, seq)
    n_steps = QKV_COLS // tn
    per_seq = seq // tm
    tab_spec = pl.BlockSpec((tm, LANES), lambda i, j: (i % per_seq, 0))
    est = (2 * tm * d * 4 + 2 * tm * d * 2 + 2 * d * tn * 2 + 8 * tm * LANES * 4 + 2 * tm * tn * 2
           + 3 * tm * tn * 4)
    return pl.pallas_call(
        functools.partial(_inproj_qkv_kernel, tn=tn, tc=tc, n_steps=n_steps),
        grid=(t // tm, n_steps),
        in_specs=[
            pl.BlockSpec((tm, d), lambda i, j: (i, 0)),
            pl.BlockSpec((d, tn), lambda i, j: (0, j)),
            tab_spec, tab_spec, tab_spec, tab_spec,
            pl.BlockSpec((1, tn), lambda i, j: (0, j)),
        ],
        out_specs=[
            pl.BlockSpec((tm, tn), lambda i, j: (i, j)),
            pl.BlockSpec((tm, d), lambda i, j: (i, 0)),
        ],
        out_shape=[jax.ShapeDtypeStruct((t, QKV_COLS), BF16), jax.ShapeDtypeStruct((t, d), BF16)],
        compiler_params=pltpu.CompilerParams(
            dimension_semantics=("arbitrary", "arbitrary"), vmem_limit_bytes=_vmem_limit(est)),
        name="inproj_qkv",
    )(x2d, w_qkv, *tabs, colscale)


def _inproj_gates_kernel(x_ref, w_ref, bias_ref, g_ref, *, tc):
    n = w_ref.shape[1] // tc
    ys = [jnp.dot(x_ref[...], w_ref[:, c * tc:(c + 1) * tc], preferred_element_type=F32) for c in range(n)]
    for c in range(n):
        cols = slice(c * tc, (c + 1) * tc)
        g_ref[:, cols] = (0.5 * jnp.tanh(0.5 * (ys[c] + bias_ref[:, cols])) + 0.5).astype(BF16)


def _inproj_gates(xb, w_gate, bias, tm=2048, tn=1024, tc=256):
    t, d = xb.shape
    cols = w_gate.shape[1]
    tm = min(tm, t)
    tn = min(tn, cols)
    est = 2 * tm * d * 2 + 2 * d * tn * 2 + 2 * tm * tn * 2 + 3 * tm * tn * 4
    return pl.pallas_call(
        functools.partial(_inproj_gates_kernel, tc=tc),
        grid=(t // tm, cols // tn),
        in_specs=[
            pl.BlockSpec((tm, d), lambda i, j: (i, 0)),
            pl.BlockSpec((d, tn), lambda i, j: (0, j)),
            pl.BlockSpec((1, tn), lambda i, j: (0, j)),
        ],
        out_specs=pl.BlockSpec((tm, tn), lambda i, j: (i, j)),
        out_shape=jax.ShapeDtypeStruct((t, cols), BF16),
        compiler_params=pltpu.CompilerParams(
            dimension_semantics=("parallel", "arbitrary"), vmem_limit_bytes=_vmem_limit(est)),
        name="inproj_gates",
    )(xb, w_gate, bias)


def _softmax_parts(q, k):
    s = lax.dot_general(q, k, (((1,), (1,)), ((), ())), preferred_element_type=F32)
    p = jnp.exp2(s - jnp.max(s, axis=-1, keepdims=True))
    return p, jnp.sum(p, axis=-1, keepdims=True)


def _attn_a_kernel(q_ref, k_ref, v_ref, o_ref, *, tq, unroll):
    nq = q_ref.shape[0] // tq

    def body(qi, carry):
        r = pl.multiple_of(qi * tq, tq)
        for g in range(A_GROUP):
            sl = slice(g * HEAD_DIM, (g + 1) * HEAD_DIM)
            p, l = _softmax_parts(q_ref[pl.ds(r, tq), sl], k_ref[...])
            o = jnp.dot(p.astype(BF16), v_ref[...], preferred_element_type=F32)
            o_ref[pl.ds(r, tq), sl] = (o / l).astype(BF16)
        return carry

    lax.fori_loop(0, nq, body, 0, unroll=unroll)


def _attn_a(qkv, bsz, seq, tq=256, unroll=4):
    t = qkv.shape[0]
    gw = A_GROUP * HEAD_DIM
    tq = min(tq, seq)
    unroll = min(unroll, seq // tq)
    est = 2 * (2 * seq * gw * 2 + 2 * seq * HEAD_DIM * 2) + 4 * unroll * tq * seq * 4
    return pl.pallas_call(
        functools.partial(_attn_a_kernel, tq=tq, unroll=unroll),
        grid=(bsz, A_KV_HEADS),
        in_specs=[
            pl.BlockSpec((seq, gw), lambda b, h: (b, h)),
            pl.BlockSpec((seq, HEAD_DIM), lambda b, h: (b, OFF_A_K // HEAD_DIM + h)),
            pl.BlockSpec((seq, HEAD_DIM), lambda b, h: (b, OFF_A_V // HEAD_DIM + h)),
        ],
        out_specs=pl.BlockSpec((seq, gw), lambda b, h: (b, h)),
        out_shape=jax.ShapeDtypeStruct((t, A_WIDTH), BF16),
        compiler_params=pltpu.CompilerParams(
            dimension_semantics=("parallel", "parallel"), vmem_limit_bytes=_vmem_limit(est)),
        name="attn_a",
    )(qkv, qkv, qkv)


def _attn_b_kernel(lam_ref, sub_ref, q_ref, k_ref, v_ref, o_ref, *, tq, lam_init, unroll):
    nq = q_ref.shape[0] // tq
    lp = lam_ref[...]
    lam = (jnp.exp(jnp.sum(lp[0:1] * lp[1:2], axis=-1, keepdims=True))
           - jnp.exp(jnp.sum(lp[2:3] * lp[3:4], axis=-1, keepdims=True)) + lam_init)
    d = B_QK_DIM

    def body(qi, carry):
        r = pl.multiple_of(qi * tq, tq)
        p1, l1 = _softmax_parts(q_ref[pl.ds(r, tq), 0:d], k_ref[:, 0:d])
        p2, l2 = _softmax_parts(q_ref[pl.ds(r, tq), d:2 * d], k_ref[:, d:2 * d])
        o1 = jnp.dot(p1.astype(BF16), v_ref[...], preferred_element_type=F32)
        o2 = jnp.dot(p2.astype(BF16), v_ref[...], preferred_element_type=F32)
        o = o1 * (1.0 / l1) - o2 * (lam / l2)
        ms = jnp.mean(o * o, axis=-1, keepdims=True)
        o = o * lax.rsqrt(ms + RMS_EPS) * sub_ref[...] * (1.0 - lam_init)
        o_ref[pl.ds(r, tq), :] = o.astype(BF16)
        return carry

    lax.fori_loop(0, nq, body, 0, unroll=unroll)


def _attn_b(qkv, lam_p, subln, bsz, seq, lam_init, tq=256, unroll=8):
    t = qkv.shape[0]
    tq = min(tq, seq)
    unroll = min(unroll, seq // tq)
    w = 2 * B_QK_DIM
    est = 2 * 4 * seq * w * 2 + 5 * unroll * tq * seq * 4
    return pl.pallas_call(
        functools.partial(_attn_b_kernel, tq=tq, lam_init=lam_init, unroll=unroll),
        grid=(bsz, B_HEADS),
        in_specs=[
            pl.BlockSpec((4, B_QK_DIM), lambda b, h: (0, 0)),
            pl.BlockSpec((1, B_V_DIM), lambda b, h: (0, 0)),
            pl.BlockSpec((seq, w), lambda b, h: (b, OFF_B_Q // w + h)),
            pl.BlockSpec((seq, w), lambda b, h: (b, OFF_B_K // w + h)),
            pl.BlockSpec((seq, B_V_DIM), lambda b, h: (b, OFF_B_V // B_V_DIM + h)),
        ],
        out_specs=pl.BlockSpec((seq, B_V_DIM), lambda b, h: (b, h)),
        out_shape=jax.ShapeDtypeStruct((t, B_WIDTH), BF16),
        compiler_params=pltpu.CompilerParams(
            dimension_semantics=("parallel", "parallel"), vmem_limit_bytes=_vmem_limit(est)),
        name="attn_b",
    )(lam_p, subln, qkv, qkv, qkv)


def _layer_norm(v, g, b):
    mu = jnp.mean(v, axis=-1, keepdims=True)
    c = v - mu
    var = jnp.mean(c * c, axis=-1, keepdims=True)
    return c * lax.rsqrt(var + LN_EPS) * g + b


def _merge_kernel(oa_ref, ob_ref, g0_ref, g1_ref, wa_ref, wb_ref, m_ref):
    ya = jnp.dot(oa_ref[...], wa_ref[...], preferred_element_type=F32)
    yb = jnp.dot(ob_ref[...], wb_ref[...], preferred_element_type=F32)
    m_ref[...] = (g0_ref[...].astype(F32) * ya + g1_ref[...].astype(F32) * yb).astype(BF16)


def _merge(oa, ob, g, wa, wb, tm=1024, tn=1024):
    t = oa.shape[0]
    d = wa.shape[1]
    tm = min(tm, t)
    tn = min(tn, d)
    nn = d // tn
    est = (2 * 2 * tm * A_WIDTH * 2 + 2 * 2 * A_WIDTH * tn * 2 + 2 * 2 * tm * tn * 2 + 2 * tm * tn * 2
           + 4 * tm * tn * 4)
    return pl.pallas_call(
        _merge_kernel,
        grid=(t // tm, nn),
        in_specs=[
            pl.BlockSpec((tm, A_WIDTH), lambda i, n: (i, 0)),
            pl.BlockSpec((tm, B_WIDTH), lambda i, n: (i, 0)),
            pl.BlockSpec((tm, tn), lambda i, n: (i, n)),
            pl.BlockSpec((tm, tn), lambda i, n: (i, nn + n)),
            pl.BlockSpec((A_WIDTH, tn), lambda i, n: (0, n)),
            pl.BlockSpec((B_WIDTH, tn), lambda i, n: (0, n)),
        ],
        out_specs=pl.BlockSpec((tm, tn), lambda i, n: (i, n)),
        out_shape=jax.ShapeDtypeStruct((t, d), BF16),
        compiler_params=pltpu.CompilerParams(
            dimension_semantics=("parallel", "arbitrary"), vmem_limit_bytes=_vmem_limit(est)),
        name="merge",
    )(oa, ob, g, g, wa, wb)


def _outproj_kernel(m_ref, x_ref, wo_ref, lg_ref, lb_ref, wr_ref, x1_ref, x1b_ref, aff_ref, *, alpha):
    ne = aff_ref.shape[1]
    mix = jnp.dot(m_ref[...], wo_ref[...], preferred_element_type=F32)
    x1 = _layer_norm(alpha * x_ref[...] + mix, lg_ref[...], lb_ref[...])
    x1_ref[...] = x1
    hi = x1.astype(BF16)
    x1b_ref[...] = hi
    lo = (x1 - hi.astype(F32)).astype(BF16)
    both = jnp.dot(hi, wr_ref[...], preferred_element_type=F32)
    low = jnp.dot(lo, wr_ref[:, 0:ne], preferred_element_type=F32)
    logits = both[:, 0:ne] + (both[:, ne:2 * ne] + low)
    e = jnp.exp(logits - jnp.max(logits, axis=-1, keepdims=True))
    aff_ref[...] = e / jnp.sum(e, axis=-1, keepdims=True)


def _outproj(merged, x2d, wo, ln_g, ln_b, wr2, alpha, tm=512):
    t, d = x2d.shape
    ne = wr2.shape[1] // 2
    tm = min(tm, t)
    const = lambda shape: pl.BlockSpec(shape, lambda i: (0, 0), pipeline_mode=pl.Buffered(1))
    row = lambda width: pl.BlockSpec((tm, width), lambda i: (i, 0))
    est = d * d * 2 + 2 * tm * d * 2 + 2 * tm * d * 4 + 2 * tm * d * 6 + 5 * tm * d * 4 + d * LANES * 2
    return pl.pallas_call(
        functools.partial(_outproj_kernel, alpha=alpha),
        grid=(t // tm,),
        in_specs=[row(d), row(d), const((d, d)), const((1, d)), const((1, d)), const((d, 2 * ne))],
        out_specs=[row(d), row(d), row(ne)],
        out_shape=[
            jax.ShapeDtypeStruct((t, d), F32),
            jax.ShapeDtypeStruct((t, d), BF16),
            jax.ShapeDtypeStruct((t, ne), F32),
        ],
        compiler_params=pltpu.CompilerParams(
            dimension_semantics=("parallel",), vmem_limit_bytes=_vmem_limit(est)),
        name="outproj",
    )(merged, x2d, wo, ln_g, ln_b, wr2)


def _exclusive_prefix_chunks(mask_chunks, tri):
    out = []
    offset = None
    for m in mask_chunks:
        mf = m.astype(F32)
        incl = jnp.dot(m.astype(BF16), tri, preferred_element_type=F32)
        excl = incl - mf
        out.append(excl if offset is None else excl + offset)
        total = incl[:, LANES - 1:LANES]
        offset = total if offset is None else offset + total
    return out


def _topk_kernel(aff_ref, pos_ref, *, cap):
    ne, seq = aff_ref.shape
    aff = aff_ref[...]

    def search(i, thr):
        cand = thr | lax.shift_left(jnp.int32(1), 30 - i)
        cnt = jnp.sum((aff >= pltpu.bitcast(cand, F32)).astype(jnp.int32), axis=-1, keepdims=True)
        return jnp.where(cnt >= cap, cand, thr)

    thr = lax.fori_loop(0, 31, search, jnp.zeros((ne, 1), jnp.int32))
    gt = aff >= pltpu.bitcast(thr + 1, F32)
    eq = (aff >= pltpu.bitcast(thr, F32)) & jnp.logical_not(gt)
    need = cap - jnp.sum(gt.astype(jnp.int32), axis=-1, keepdims=True)
    tri = (lax.broadcasted_iota(jnp.int32, (LANES, LANES), 0)
           <= lax.broadcasted_iota(jnp.int32, (LANES, LANES), 1)).astype(BF16)
    chunks = [slice(c * LANES, (c + 1) * LANES) for c in range(seq // LANES)]
    eq_rank = _exclusive_prefix_chunks([eq[:, c] for c in chunks], tri)
    sel = [gt[:, c] | (eq[:, c] & (r < need.astype(F32))) for c, r in zip(chunks, eq_rank)]
    slot = _exclusive_prefix_chunks(sel, tri)
    for c, s, p in zip(chunks, sel, slot):
        pos_ref[:, c] = jnp.where(s, p.astype(jnp.int32), -1)


def _topk(aff_t, cap):
    bsz, ne, seq = aff_t.shape
    return pl.pallas_call(
        functools.partial(_topk_kernel, cap=cap),
        grid=(bsz,),
        in_specs=[pl.BlockSpec((None, ne, seq), lambda b: (b, 0, 0))],
        out_specs=pl.BlockSpec((None, ne, seq), lambda b: (b, 0, 0)),
        out_shape=jax.ShapeDtypeStruct((bsz, ne, seq), jnp.int32),
        compiler_params=pltpu.CompilerParams(dimension_semantics=("parallel",)),
        name="topk",
    )(aff_t)


def _gather_kernel(pos_ref, aff_ref, x_ref, xg_ref, gs_ref, *, cap):
    seq = x_ref.shape[0]
    slot = lax.broadcasted_iota(jnp.int32, (cap, seq), 0)
    for e in range(pos_ref.shape[0]):
        hit = pos_ref[e] == slot
        xg = jnp.dot(jnp.where(hit, 1.0, 0.0).astype(BF16), x_ref[...], preferred_element_type=F32)
        xg_ref[e] = xg.astype(BF16)
        gs_ref[e] = jnp.sum(jnp.where(hit, aff_ref[e], 0.0), axis=-1, keepdims=True)


def _gather(pos4, aff4, x1b, cap, eg=8):
    bsz, ne, _, seq = pos4.shape
    d = x1b.shape[1]
    eg = min(eg, ne)
    row_spec = pl.BlockSpec((None, eg, 1, seq), lambda b, e: (b, e, 0, 0))
    est = 2 * seq * d * 2 + 2 * eg * cap * d * 2 + 4 * cap * seq * 4 + 2 * cap * d * 4
    return pl.pallas_call(
        functools.partial(_gather_kernel, cap=cap),
        grid=(bsz, ne // eg),
        in_specs=[row_spec, row_spec, pl.BlockSpec((seq, d), lambda b, e: (b, 0))],
        out_specs=[
            pl.BlockSpec((eg, cap, d), lambda b, e: (e, b, 0)),
            pl.BlockSpec((eg, cap, 1), lambda b, e: (e, b, 0)),
        ],
        out_shape=[
            jax.ShapeDtypeStruct((ne, bsz * cap, d), BF16),
            jax.ShapeDtypeStruct((ne, bsz * cap, 1), F32),
        ],
        compiler_params=pltpu.CompilerParams(
            dimension_semantics=("parallel", "arbitrary"), vmem_limit_bytes=_vmem_limit(est)),
        name="gather",
    )(pos4, aff4, x1b)


def _experts_kernel(xg_ref, gs_ref, wg_ref, wu_ref, wd_ref, og_ref, acc_ref):
    f = pl.program_id(2)

    @pl.when(f == 0)
    def _():
        acc_ref[...] = jnp.zeros_like(acc_ref)

    x = xg_ref[...]
    hg = jnp.dot(x, wg_ref[...].astype(BF16), preferred_element_type=F32)
    hu = jnp.dot(x, wu_ref[...].astype(BF16), preferred_element_type=F32)
    hid = hg * (0.5 * jnp.tanh(0.5 * hg) + 0.5) * hu
    acc_ref[...] += jnp.dot(hid.astype(BF16), wd_ref[...].astype(BF16), preferred_element_type=F32)

    @pl.when(f == pl.num_programs(2) - 1)
    def _():
        og_ref[...] = (acc_ref[...] * gs_ref[...]).astype(BF16)


def _experts(xg, gs, w_gate, w_up, w_down, tmx=1024, tf=512):
    ne, rows, d = xg.shape
    ff = w_gate.shape[2]
    tmx = min(tmx, rows)
    tf = min(tf, ff)
    est = (2 * tmx * d * 2 + 2 * tmx * LANES * 4 + 2 * 3 * d * tf * 4 + 2 * tmx * d * 2 + tmx * d * 4
           + 3 * d * tf * 2 + 4 * tmx * tf * 4 + tmx * d * 4)
    return pl.pallas_call(
        _experts_kernel,
        grid=(ne, rows // tmx, ff // tf),
        in_specs=[
            pl.BlockSpec((None, tmx, d), lambda e, m, f: (e, m, 0)),
            pl.BlockSpec((None, tmx, 1), lambda e, m, f: (e, m, 0)),
            pl.BlockSpec((None, d, tf), lambda e, m, f: (e, 0, f)),
            pl.BlockSpec((None, d, tf), lambda e, m, f: (e, 0, f)),
            pl.BlockSpec((None, tf, d), lambda e, m, f: (e, f, 0)),
        ],
        out_specs=pl.BlockSpec((None, tmx, d), lambda e, m, f: (e, m, 0)),
        out_shape=jax.ShapeDtypeStruct((ne, rows, d), BF16),
        scratch_shapes=[pltpu.VMEM((tmx, d), F32)],
        compiler_params=pltpu.CompilerParams(
            dimension_semantics=("parallel", "parallel", "arbitrary"), vmem_limit_bytes=_vmem_limit(est)),
        name="experts",
    )(xg, gs, w_gate, w_up, w_down)


def _combine_kernel(pos_ref, og_ref, x1_ref, lg_ref, lb_ref, o_ref, *, cap, alpha):
    ts = x1_ref.shape[0]
    slot = lax.broadcasted_iota(jnp.int32, (cap, ts), 0)
    y = None
    for e in range(og_ref.shape[0]):
        onehot = jnp.where(pos_ref[e] == slot, 1.0, 0.0).astype(BF16)
        part = lax.dot_general(onehot, og_ref[e], (((0,), (0,)), ((), ())), preferred_element_type=F32)
        y = part if y is None else y + part
    o_ref[...] = _layer_norm(alpha * x1_ref[...] + y, lg_ref[...], lb_ref[...])


def _combine(pos4, og, x1, ln_g, ln_b, cap, alpha, ts=256):
    bsz, ne, _, seq = pos4.shape
    t, d = x1.shape
    ts = min(ts, seq)
    nth = seq // ts
    est = 2 * ne * cap * d * 2 + 4 * ts * d * 4 + 6 * ts * d * 4 + 4 * cap * ts * 4
    return pl.pallas_call(
        functools.partial(_combine_kernel, cap=cap, alpha=alpha),
        grid=(bsz, nth),
        in_specs=[
            pl.BlockSpec((None, ne, 1, ts), lambda b, h: (b, 0, 0, h)),
            pl.BlockSpec((ne, cap, d), lambda b, h: (0, b, 0)),
            pl.BlockSpec((ts, d), lambda b, h: (b * nth + h, 0)),
            pl.BlockSpec((1, d), lambda b, h: (0, 0)),
            pl.BlockSpec((1, d), lambda b, h: (0, 0)),
        ],
        out_specs=pl.BlockSpec((ts, d), lambda b, h: (b * nth + h, 0)),
        out_shape=jax.ShapeDtypeStruct((t, d), F32),
        compiler_params=pltpu.CompilerParams(
            dimension_semantics=("parallel", "arbitrary"), vmem_limit_bytes=_vmem_limit(est)),
        name="combine",
    )(pos4, og, x1, ln_g, ln_b)


def kernel(x, w_in, b_gate, a_q_norm, a_k_norm, b_lambda, b_subln, w_a_proj, w_b_proj, w_o, ln1_g, ln1_b,
           w_router, w_gate, w_up, w_down, ln2_g, ln2_b):
    bsz, seq, d = x.shape
    depth = w_in.shape[0]
    ne = w_router.shape[2]
    cap = CAPACITY_FACTOR * seq // ne
    alpha = (2.0 * depth) ** 0.25
    qscale = HEAD_DIM ** -0.5 * LOG2E
    tabs = _rope_tables(seq)
    ones = lambda n: jnp.ones((n,), F32)

    x2d = x.reshape(bsz * seq, d)
    for l in range(depth):
        lam_init = 0.8 - 0.6 * math.exp(-0.3 * l)
        colscale = jnp.concatenate([
            jnp.tile(a_q_norm[l][_PERM_A], A_Q_HEADS) * qscale, jnp.tile(a_k_norm[l][_PERM_A], A_KV_HEADS),
            ones(COL_A_V), ones(COL_B_Q) * qscale, ones(COL_B_K + COL_B_V)])[None, :]
        w_qkv, w_g = _prep_w(w_in[l])
        qkv, xb = _inproj_qkv(x2d, w_qkv, tabs, colscale, seq)
        gates = _inproj_gates(xb, w_g, b_gate[l][None, :])
        oa = _attn_a(qkv, bsz, seq)
        ob = _attn_b(qkv, b_lambda[l], b_subln[l][None, :], bsz, seq, lam_init)
        merged = _merge(oa, ob, gates, w_a_proj[l].astype(BF16), w_b_proj[l].astype(BF16))
        wr_hi = w_router[l].astype(BF16)
        wr_lo = (w_router[l] - wr_hi.astype(F32)).astype(BF16)
        x1, x1b, aff = _outproj(merged, x2d, w_o[l].astype(BF16), ln1_g[l][None, :], ln1_b[l][None, :],
                                jnp.concatenate([wr_hi, wr_lo], axis=1), alpha)
        aff_t = jnp.swapaxes(aff.reshape(bsz, seq, ne), 1, 2)
        pos = _topk(aff_t, cap)
        pos4 = pos.reshape(bsz, ne, 1, seq)
        xg, gs = _gather(pos4, aff_t.reshape(bsz, ne, 1, seq), x1b, cap)
        og = _experts(xg, gs, w_gate[l], w_up[l], w_down[l])
        x2d = _combine(pos4, og, x1, ln2_g[l][None, :], ln2_b[l][None, :], cap, alpha)
    return x2d.reshape(bsz, seq, d)
```

```python
import functools
import math

import numpy as np

import jax
import jax.numpy as jnp
from jax import lax
from jax.experimental import pallas as pl
from jax.experimental.pallas import tpu as pltpu

F32 = jnp.float32
BF16 = jnp.bfloat16

HEAD_DIM = 128
A_Q_HEADS = 8
A_KV_HEADS = 2
A_GROUP = A_Q_HEADS // A_KV_HEADS
A_ROPE_THETA = 10000.0
A_WIDTH = A_Q_HEADS * HEAD_DIM
B_HEADS = 4
B_QK_DIM = 128
B_V_DIM = 2 * B_QK_DIM
B_WIDTH = B_HEADS * B_V_DIM
PARTIAL_ROPE_THETA = 500000.0
PARTIAL_ROPE_DIMS = B_QK_DIM // 4
GRID_W = 64
CAPACITY_FACTOR = 2
RMS_EPS = 1e-6
LN_EPS = 1e-5

COL_A_Q = A_Q_HEADS * HEAD_DIM
COL_A_K = A_KV_HEADS * HEAD_DIM
COL_A_V = A_KV_HEADS * HEAD_DIM
COL_B_Q = B_HEADS * 2 * B_QK_DIM
COL_B_K = B_HEADS * 2 * B_QK_DIM
COL_B_V = B_HEADS * B_V_DIM
QKV_COLS = COL_A_Q + COL_A_K + COL_A_V + COL_B_Q + COL_B_K + COL_B_V
OFF_A_K = COL_A_Q
OFF_A_V = OFF_A_K + COL_A_K
OFF_B_Q = OFF_A_V + COL_A_V
OFF_B_K = OFF_B_Q + COL_B_Q
OFF_B_V = OFF_B_K + COL_B_K

LANES = 128
V7X_VMEM_BYTES = 64 * 1024 * 1024
LOG2E = math.log2(math.e)


def _vmem_limit(estimate_bytes):
    return int(min(estimate_bytes * 5 // 4 + (2 << 20), V7X_VMEM_BYTES - (6 << 20)))


_QUARTER = HEAD_DIM // 4
_PERM_A = np.concatenate([np.arange(0, _QUARTER), np.arange(2 * _QUARTER, 3 * _QUARTER),
                          np.arange(_QUARTER, 2 * _QUARTER), np.arange(3 * _QUARTER, HEAD_DIM)])
_HALF_B = PARTIAL_ROPE_DIMS // 2
_PERM_B = np.arange(B_QK_DIM)
_PERM_B[_HALF_B:2 * _HALF_B] = np.arange(LANES // 2, LANES // 2 + _HALF_B)
_PERM_B[LANES // 2:LANES // 2 + _HALF_B] = np.arange(_HALF_B, 2 * _HALF_B)


def _col_kind(col):
    if col < OFF_A_V:
        return "norm_rope"
    if col < OFF_B_Q:
        return "plain"
    if col < OFF_B_V:
        return "rope"
    if col < QKV_COLS:
        return "plain"
    return "gate"


def _runs(kinds):
    out, start = [], 0
    for i in range(1, len(kinds) + 1):
        if i == len(kinds) or kinds[i] != kinds[start]:
            out.append((start, i, kinds[start]))
            start = i
    return out


def _swap_lane_blocks(w, lo, hi, width):
    lane = lax.broadcasted_iota(jnp.int32, w.shape, 1)
    up = pltpu.roll(w, LANES - (hi - lo), 1)
    down = pltpu.roll(w, hi - lo, 1)
    w = jnp.where((lane >= lo) & (lane < lo + width), up, w)
    return jnp.where((lane >= hi) & (lane < hi + width), down, w)


def _prep_w_kernel(w_ref, oq_ref, og_ref, *, tn, n_qkv, n_steps):
    j = pl.program_id(0)
    kinds = [tuple(_col_kind(s * tn + h * LANES) for h in range(tn // LANES)) for s in range(n_steps)]
    for start, stop, ks in _runs(kinds):
        @pl.when((j >= start) & (j < stop))
        def _(ks=ks, o_ref=oq_ref if start < n_qkv else og_ref):
            for h, kind in enumerate(ks):
                sl = slice(h * LANES, (h + 1) * LANES)
                w = w_ref[:, sl]
                if kind == "norm_rope":
                    w = _swap_lane_blocks(w, _QUARTER, 2 * _QUARTER, _QUARTER)
                elif kind == "rope":
                    w = _swap_lane_blocks(w, _HALF_B, LANES // 2, _HALF_B)
                o_ref[:, sl] = w.astype(BF16)


def _prep_w(w, tn=512):
    d, cols = w.shape
    n_qkv = QKV_COLS // tn
    n_g = (cols - QKV_COLS) // tn
    return pl.pallas_call(
        functools.partial(_prep_w_kernel, tn=tn, n_qkv=n_qkv, n_steps=n_qkv + n_g),
        grid=(n_qkv + n_g,),
        in_specs=[pl.BlockSpec((d, tn), lambda j: (0, j))],
        out_specs=[
            pl.BlockSpec((d, tn), lambda j: (0, jnp.minimum(j, n_qkv - 1))),
            pl.BlockSpec((d, tn), lambda j: (0, jnp.clip(j - n_qkv, 0, n_g - 1))),
        ],
        out_shape=[jax.ShapeDtypeStruct((d, QKV_COLS), BF16), jax.ShapeDtypeStruct((d, cols - QKV_COLS), BF16)],
        compiler_params=pltpu.CompilerParams(
            dimension_semantics=("arbitrary",), vmem_limit_bytes=_vmem_limit(2 * d * tn * 8 + 2 * d * tn * 4)),
        name="prep_w",
    )(w)


def _rope_tables(seq):
    pos = jnp.arange(seq)

    def cos_sin(p, dim, theta):
        inv = theta ** (-jnp.arange(0, dim, 2, dtype=F32) / dim)
        ang = p.astype(F32)[:, None] * inv[None, :]
        return jnp.cos(ang), jnp.sin(ang)

    cr, sr = cos_sin(pos // GRID_W, HEAD_DIM // 2, A_ROPE_THETA)
    cc, sc = cos_sin(pos % GRID_W, HEAD_DIM // 2, A_ROPE_THETA)
    cl, sl = cos_sin(pos, PARTIAL_ROPE_DIMS, PARTIAL_ROPE_THETA)
    one = jnp.ones((seq, LANES // 2 - _HALF_B), F32)
    zero = jnp.zeros((seq, LANES // 2 - _HALF_B), F32)
    cos_a = jnp.concatenate([cr, cc, cr, cc], axis=-1)
    sin_a = jnp.concatenate([-sr, -sc, sr, sc], axis=-1)
    cos_b = jnp.concatenate([cl, one, cl, one], axis=-1)
    sin_b = jnp.concatenate([-sl, zero, sl, zero], axis=-1)
    return cos_a, sin_a, cos_b, sin_b


def _inproj_qkv_kernel(x_ref, w_ref, ca_ref, sa_ref, cb_ref, sb_ref, cs_ref, qkv_ref, xb_ref, *, tn, tc, n_steps):
    j = pl.program_id(1)

    @pl.when(j == 0)
    def _():
        xb_ref[...] = x_ref[...].astype(BF16)

    def epilogue(c, kind, y):
        cols = slice(c * tc, (c + 1) * tc)
        if kind == "plain":
            qkv_ref[:, cols] = y.astype(BF16)
            return
        if kind == "norm_rope":
            head = lambda ax: lax.broadcasted_iota(jnp.int32, (tc, tc), ax) // HEAD_DIM
            blockdiag = (head(0) == head(1)).astype(BF16)
            ss = jnp.dot((y * y).astype(BF16), blockdiag, preferred_element_type=F32)
            z = y * lax.rsqrt(ss * (1.0 / HEAD_DIM) + RMS_EPS) * cs_ref[:, cols]
            cos_ref, sin_ref = ca_ref, sa_ref
        else:
            z = y * cs_ref[:, cols]
            cos_ref, sin_ref = cb_ref, sb_ref
        for h in range(tc // LANES):
            zh = z[:, h * LANES:(h + 1) * LANES]
            out = zh * cos_ref[...] + pltpu.roll(zh, LANES // 2, 1) * sin_ref[...]
            qkv_ref[:, c * tc + h * LANES:c * tc + (h + 1) * LANES] = out.astype(BF16)

    kinds = [tuple(_col_kind(s * tn + c * tc) for c in range(tn // tc)) for s in range(n_steps)]
    for start, stop, ks in _runs(kinds):
        @pl.when((j >= start) & (j < stop))
        def _(ks=ks):
            ys = [jnp.dot(xb_ref[...], w_ref[:, c * tc:(c + 1) * tc], preferred_element_type=F32)
                  for c in range(len(ks))]
            for c, kind in enumerate(ks):
                epilogue(c, kind, ys[c])


def _inproj_qkv(x2d, w_qkv, tabs, colscale, seq, tm=512, tn=1536, tc=256):
    t, d = x2d.shape
    tm = min(tm, seq)
    n_steps = QKV_COLS // tn
    per_seq = seq // tm
    tab_spec = pl.BlockSpec((tm, LANES), lambda i, j: (i % per_seq, 0))
    est = (2 * tm * d * 4 + 2 * tm * d * 2 + 2 * d * tn * 2 + 8 * tm * LANES * 4 + 2 * tm * tn * 2
           + 3 * tm * tn * 4)
    return pl.pallas_call(
        functools.partial(_inproj_qkv_kernel, tn=tn, tc=tc, n_steps=n_steps),
        grid=(t // tm, n_steps),
        in_specs=[
            pl.BlockSpec((tm, d), lambda i, j: (i, 0)),
            pl.BlockSpec((d, tn), lambda i, j: (0, j)),
            tab_spec, tab_spec, tab_spec, tab_spec,
            pl.BlockSpec((1, tn), lambda i, j: (0, j)),
        ],
        out_specs=[
            pl.BlockSpec((tm, tn), lambda i, j: (i, j)),
            pl.BlockSpec((tm, d), lambda i, j: (i, 0)),
        ],
        out_shape=[jax.ShapeDtypeStruct((t, QKV_COLS), BF16), jax.ShapeDtypeStruct((t, d), BF16)],
        compiler_params=pltpu.CompilerParams(
            dimension_semantics=("arbitrary", "arbitrary"), vmem_limit_bytes=_vmem_limit(est)),
        name="inproj_qkv",
    )(x2d, w_qkv, *tabs, colscale)


def _inproj_gates_kernel(x_ref, w_ref, bias_ref, g_ref, *, tc):
    n = w_ref.shape[1] // tc
    ys = [jnp.dot(x_ref[...], w_ref[:, c * tc:(c + 1) * tc], preferred_element_type=F32) for c in range(n)]
    for c in range(n):
        cols = slice(c * tc, (c + 1) * tc)
        g_ref[:, cols] = (0.5 * jnp.tanh(0.5 * (ys[c] + bias_ref[:, cols])) + 0.5).astype(BF16)


def _inproj_gates(xb, w_gate, bias, tm=2048, tn=1024, tc=256):
    t, d = xb.shape
    cols = w_gate.shape[1]
    tm = min(tm, t)
    tn = min(tn, cols)
    est = 2 * tm * d * 2 + 2 * d * tn * 2 + 2 * tm * tn * 2 + 3 * tm * tn * 4
    return pl.pallas_call(
        functools.partial(_inproj_gates_kernel, tc=tc),
        grid=(t // tm, cols // tn),
        in_specs=[
            pl.BlockSpec((tm, d), lambda i, j: (i, 0)),
            pl.BlockSpec((d, tn), lambda i, j: (0, j)),
            pl.BlockSpec((1, tn), lambda i, j: (0, j)),
        ],
        out_specs=pl.BlockSpec((tm, tn), lambda i, j: (i, j)),
        out_shape=jax.ShapeDtypeStruct((t, cols), BF16),
        compiler_params=pltpu.CompilerParams(
            dimension_semantics=("parallel", "arbitrary"), vmem_limit_bytes=_vmem_limit(est)),
        name="inproj_gates",
    )(xb, w_gate, bias)


def _softmax_parts(q, k):
    s = lax.dot_general(q, k, (((1,), (1,)), ((), ())), preferred_element_type=F32)
    p = jnp.exp2(s - jnp.max(s, axis=-1, keepdims=True))
    return p, jnp.sum(p, axis=-1, keepdims=True)


def _attn_a_kernel(q_ref, k_ref, v_ref, o_ref, *, tq, unroll):
    nq = q_ref.shape[0] // tq

    def body(qi, carry):
        r = pl.multiple_of(qi * tq, tq)
        for g in range(A_GROUP):
            sl = slice(g * HEAD_DIM, (g + 1) * HEAD_DIM)
            p, l = _softmax_parts(q_ref[pl.ds(r, tq), sl], k_ref[...])
            o = jnp.dot(p.astype(BF16), v_ref[...], preferred_element_type=F32)
            o_ref[pl.ds(r, tq), sl] = (o / l).astype(BF16)
        return carry

    lax.fori_loop(0, nq, body, 0, unroll=unroll)


def _attn_a(qkv, bsz, seq, tq=256, unroll=4):
    t = qkv.shape[0]
    gw = A_GROUP * HEAD_DIM
    tq = min(tq, seq)
    unroll = min(unroll, seq // tq)
    est = 2 * (2 * seq * gw * 2 + 2 * seq * HEAD_DIM * 2) + 4 * unroll * tq * seq * 4
    return pl.pallas_call(
        functools.partial(_attn_a_kernel, tq=tq, unroll=unroll),
        grid=(bsz, A_KV_HEADS),
        in_specs=[
            pl.BlockSpec((seq, gw), lambda b, h: (b, h)),
            pl.BlockSpec((seq, HEAD_DIM), lambda b, h: (b, OFF_A_K // HEAD_DIM + h)),
            pl.BlockSpec((seq, HEAD_DIM), lambda b, h: (b, OFF_A_V // HEAD_DIM + h)),
        ],
        out_specs=pl.BlockSpec((seq, gw), lambda b, h: (b, h)),
        out_shape=jax.ShapeDtypeStruct((t, A_WIDTH), BF16),
        compiler_params=pltpu.CompilerParams(
            dimension_semantics=("parallel", "parallel"), vmem_limit_bytes=_vmem_limit(est)),
        name="attn_a",
    )(qkv, qkv, qkv)


def _attn_b_kernel(lam_ref, sub_ref, q_ref, k_ref, v_ref, o_ref, *, tq, lam_init, unroll):
    nq = q_ref.shape[0] // tq
    lp = lam_ref[...]
    lam = (jnp.exp(jnp.sum(lp[0:1] * lp[1:2], axis=-1, keepdims=True))
           - jnp.exp(jnp.sum(lp[2:3] * lp[3:4], axis=-1, keepdims=True)) + lam_init)
    d = B_QK_DIM

    def body(qi, carry):
        r = pl.multiple_of(qi * tq, tq)
        p1, l1 = _softmax_parts(q_ref[pl.ds(r, tq), 0:d], k_ref[:, 0:d])
        p2, l2 = _softmax_parts(q_ref[pl.ds(r, tq), d:2 * d], k_ref[:, d:2 * d])
        o1 = jnp.dot(p1.astype(BF16), v_ref[...], preferred_element_type=F32)
        o2 = jnp.dot(p2.astype(BF16), v_ref[...], preferred_element_type=F32)
        o = o1 * (1.0 / l1) - o2 * (lam / l2)
        ms = jnp.mean(o * o, axis=-1, keepdims=True)
        o = o * lax.rsqrt(ms + RMS_EPS) * sub_ref[...] * (1.0 - lam_init)
        o_ref[pl.ds(r, tq), :] = o.astype(BF16)
        return carry

    lax.fori_loop(0, nq, body, 0, unroll=unroll)


def _attn_b(qkv, lam_p, subln, bsz, seq, lam_init, tq=256, unroll=8):
    t = qkv.shape[0]
    tq = min(tq, seq)
    unroll = min(unroll, seq // tq)
    w = 2 * B_QK_DIM
    est = 2 * 4 * seq * w * 2 + 5 * unroll * tq * seq * 4
    return pl.pallas_call(
        functools.partial(_attn_b_kernel, tq=tq, lam_init=lam_init, unroll=unroll),
        grid=(bsz, B_HEADS),
        in_specs=[
            pl.BlockSpec((4, B_QK_DIM), lambda b, h: (0, 0)),
            pl.BlockSpec((1, B_V_DIM), lambda b, h: (0, 0)),
            pl.BlockSpec((seq, w), lambda b, h: (b, OFF_B_Q // w + h)),
            pl.BlockSpec((seq, w), lambda b, h: (b, OFF_B_K // w + h)),
            pl.BlockSpec((seq, B_V_DIM), lambda b, h: (b, OFF_B_V // B_V_DIM + h)),
        ],
        out_specs=pl.BlockSpec((seq, B_V_DIM), lambda b, h: (b, h)),
        out_shape=jax.ShapeDtypeStruct((t, B_WIDTH), BF16),
        compiler_params=pltpu.CompilerParams(
            dimension_semantics=("parallel", "parallel"), vmem_limit_bytes=_vmem_limit(est)),
        name="attn_b",
    )(lam_p, subln, qkv, qkv, qkv)


def _layer_norm(v, g, b):
    mu = jnp.mean(v, axis=-1, keepdims=True)
    c = v - mu
    var = jnp.mean(c * c, axis=-1, keepdims=True)
    return c * lax.rsqrt(var + LN_EPS) * g + b


def _merge_kernel(oa_ref, ob_ref, g0_ref, g1_ref, wa_ref, wb_ref, m_ref):
    ya = jnp.dot(oa_ref[...], wa_ref[...], preferred_element_type=F32)
    yb = jnp.dot(ob_ref[...], wb_ref[...], preferred_element_type=F32)
    m_ref[...] = (g0_ref[...].astype(F32) * ya + g1_ref[...].astype(F32) * yb).astype(BF16)


def _merge(oa, ob, g, wa, wb, tm=1024, tn=1024):
    t = oa.shape[0]
    d = wa.shape[1]
    tm = min(tm, t)
    tn = min(tn, d)
    nn = d // tn
    est = (2 * 2 * tm * A_WIDTH * 2 + 2 * 2 * A_WIDTH * tn * 2 + 2 * 2 * tm * tn * 2 + 2 * tm * tn * 2
           + 4 * tm * tn * 4)
    return pl.pallas_call(
        _merge_kernel,
        grid=(t // tm, nn),
        in_specs=[
            pl.BlockSpec((tm, A_WIDTH), lambda i, n: (i, 0)),
            pl.BlockSpec((tm, B_WIDTH), lambda i, n: (i, 0)),
            pl.BlockSpec((tm, tn), lambda i, n: (i, n)),
            pl.BlockSpec((tm, tn), lambda i, n: (i, nn + n)),
            pl.BlockSpec((A_WIDTH, tn), lambda i, n: (0, n)),
            pl.BlockSpec((B_WIDTH, tn), lambda i, n: (0, n)),
        ],
        out_specs=pl.BlockSpec((tm, tn), lambda i, n: (i, n)),
        out_shape=jax.ShapeDtypeStruct((t, d), BF16),
        compiler_params=pltpu.CompilerParams(
            dimension_semantics=("parallel", "arbitrary"), vmem_limit_bytes=_vmem_limit(est)),
        name="merge",
    )(oa, ob, g, g, wa, wb)


def _outproj_kernel(m_ref, x_ref, wo_ref, lg_ref, lb_ref, wr_ref, x1_ref, x1b_ref, aff_ref, *, alpha):
    ne = aff_ref.shape[1]
    mix = jnp.dot(m_ref[...], wo_ref[...], preferred_element_type=F32)
    x1 = _layer_norm(alpha * x_ref[...] + mix, lg_ref[...], lb_ref[...])
    x1_ref[...] = x1
    hi = x1.astype(BF16)
    x1b_ref[...] = hi
    lo = (x1 - hi.astype(F32)).astype(BF16)
    both = jnp.dot(hi, wr_ref[...], preferred_element_type=F32)
    low = jnp.dot(lo, wr_ref[:, 0:ne], preferred_element_type=F32)
    logits = both[:, 0:ne] + (both[:, ne:2 * ne] + low)
    e = jnp.exp(logits - jnp.max(logits, axis=-1, keepdims=True))
    aff_ref[...] = e / jnp.sum(e, axis=-1, keepdims=True)


def _outproj(merged, x2d, wo, ln_g, ln_b, wr2, alpha, tm=512):
    t, d = x2d.shape
    ne = wr2.shape[1] // 2
    tm = min(tm, t)
    const = lambda shape: pl.BlockSpec(shape, lambda i: (0, 0), pipeline_mode=pl.Buffered(1))
    row = lambda width: pl.BlockSpec((tm, width), lambda i: (i, 0))
    est = d * d * 2 + 2 * tm * d * 2 + 2 * tm * d * 4 + 2 * tm * d * 6 + 5 * tm * d * 4 + d * LANES * 2
    return pl.pallas_call(
        functools.partial(_outproj_kernel, alpha=alpha),
        grid=(t // tm,),
        in_specs=[row(d), row(d), const((d, d)), const((1, d)), const((1, d)), const((d, 2 * ne))],
        out_specs=[row(d), row(d), row(ne)],
        out_shape=[
            jax.ShapeDtypeStruct((t, d), F32),
            jax.ShapeDtypeStruct((t, d), BF16),
            jax.ShapeDtypeStruct((t, ne), F32),
        ],
        compiler_params=pltpu.CompilerParams(
            dimension_semantics=("parallel",), vmem_limit_bytes=_vmem_limit(est)),
        name="outproj",
    )(merged, x2d, wo, ln_g, ln_b, wr2)


def _exclusive_prefix_chunks(mask_chunks, tri):
    out = []
    offset = None
    for m in mask_chunks:
        mf = m.astype(F32)
        incl = jnp.dot(m.astype(BF16), tri, preferred_element_type=F32)
        excl = incl - mf
        out.append(excl if offset is None else excl + offset)
        total = incl[:, LANES - 1:LANES]
        offset = total if offset is None else offset + total
    return out


def _topk_kernel(aff_ref, pos_ref, *, cap):
    ne, seq = aff_ref.shape
    aff = aff_ref[...]

    def search(i, thr):
        cand = thr | lax.shift_left(jnp.int32(1), 30 - i)
        cnt = jnp.sum((aff >= pltpu.bitcast(cand, F32)).astype(jnp.int32), axis=-1, keepdims=True)
        return jnp.where(cnt >= cap, cand, thr)

    thr = lax.fori_loop(0, 31, search, jnp.zeros((ne, 1), jnp.int32))
    gt = aff >= pltpu.bitcast(thr + 1, F32)
    eq = (aff >= pltpu.bitcast(thr, F32)) & jnp.logical_not(gt)
    need = cap - jnp.sum(gt.astype(jnp.int32), axis=-1, keepdims=True)
    tri = (lax.broadcasted_iota(jnp.int32, (LANES, LANES), 0)
           <= lax.broadcasted_iota(jnp.int32, (LANES, LANES), 1)).astype(BF16)
    chunks = [slice(c * LANES, (c + 1) * LANES) for c in range(seq // LANES)]
    eq_rank = _exclusive_prefix_chunks([eq[:, c] for c in chunks], tri)
    sel = [gt[:, c] | (eq[:, c] & (r < need.astype(F32))) for c, r in zip(chunks, eq_rank)]
    slot = _exclusive_prefix_chunks(sel, tri)
    for c, s, p in zip(chunks, sel, slot):
        pos_ref[:, c] = jnp.where(s, p.astype(jnp.int32), -1)


def _topk(aff_t, cap):
    bsz, ne, seq = aff_t.shape
    return pl.pallas_call(
        functools.partial(_topk_kernel, cap=cap),
        grid=(bsz,),
        in_specs=[pl.BlockSpec((None, ne, seq), lambda b: (b, 0, 0))],
        out_specs=pl.BlockSpec((None, ne, seq), lambda b: (b, 0, 0)),
        out_shape=jax.ShapeDtypeStruct((bsz, ne, seq), jnp.int32),
        compiler_params=pltpu.CompilerParams(dimension_semantics=("parallel",)),
        name="topk",
    )(aff_t)


def _gather_kernel(pos_ref, aff_ref, x_ref, xg_ref, gs_ref, *, cap):
    seq = x_ref.shape[0]
    slot = lax.broadcasted_iota(jnp.int32, (cap, seq), 0)
    for e in range(pos_ref.shape[0]):
        hit = pos_ref[e] == slot
        xg = jnp.dot(jnp.where(hit, 1.0, 0.0).astype(BF16), x_ref[...], preferred_element_type=F32)
        xg_ref[e] = xg.astype(BF16)
        gs_ref[e] = jnp.sum(jnp.where(hit, aff_ref[e], 0.0), axis=-1, keepdims=True)


def _gather(pos4, aff4, x1b, cap, eg=8):
    bsz, ne, _, seq = pos4.shape
    d = x1b.shape[1]
    eg = min(eg, ne)
    row_spec = pl.BlockSpec((None, eg, 1, seq), lambda b, e: (b, e, 0, 0))
    est = 2 * seq * d * 2 + 2 * eg * cap * d * 2 + 4 * cap * seq * 4 + 2 * cap * d * 4
    return pl.pallas_call(
        functools.partial(_gather_kernel, cap=cap),
        grid=(bsz, ne // eg),
        in_specs=[row_spec, row_spec, pl.BlockSpec((seq, d), lambda b, e: (b, 0))],
        out_specs=[
            pl.BlockSpec((eg, cap, d), lambda b, e: (e, b, 0)),
            pl.BlockSpec((eg, cap, 1), lambda b, e: (e, b, 0)),
        ],
        out_shape=[
            jax.ShapeDtypeStruct((ne, bsz * cap, d), BF16),
            jax.ShapeDtypeStruct((ne, bsz * cap, 1), F32),
        ],
        compiler_params=pltpu.CompilerParams(
            dimension_semantics=("parallel", "arbitrary"), vmem_limit_bytes=_vmem_limit(est)),
        name="gather",
    )(pos4, aff4, x1b)


def _experts_kernel(xg_ref, gs_ref, wg_ref, wu_ref, wd_ref, og_ref, acc_ref):
    f = pl.program_id(2)

    @pl.when(f == 0)
    def _():
        acc_ref[...] = jnp.zeros_like(acc_ref)

    x = xg_ref[...]
    hg = jnp.dot(x, wg_ref[...].astype(BF16), preferred_element_type=F32)
    hu = jnp.dot(x, wu_ref[...].astype(BF16), preferred_element_type=F32)
    hid = hg * (0.5 * jnp.tanh(0.5 * hg) + 0.5) * hu
    acc_ref[...] += jnp.dot(hid.astype(BF16), wd_ref[...].astype(BF16), preferred_element_type=F32)

    @pl.when(f == pl.num_programs(2) - 1)
    def _():
        og_ref[...] = (acc_ref[...] * gs_ref[...]).astype(BF16)


def _experts(xg, gs, w_gate, w_up, w_down, tmx=1024, tf=512):
    ne, rows, d = xg.shape
    ff = w_gate.shape[2]
    tmx = min(tmx, rows)
    tf = min(tf, ff)
    est = (2 * tmx * d * 2 + 2 * tmx * LANES * 4 + 2 * 3 * d * tf * 4 + 2 * tmx * d * 2 + tmx * d * 4
           + 3 * d * tf * 2 + 4 * tmx * tf * 4 + tmx * d * 4)
    return pl.pallas_call(
        _experts_kernel,
        grid=(ne, rows // tmx, ff // tf),
        in_specs=[
            pl.BlockSpec((None, tmx, d), lambda e, m, f: (e, m, 0)),
            pl.BlockSpec((None, tmx, 1), lambda e, m, f: (e, m, 0)),
            pl.BlockSpec((None, d, tf), lambda e, m, f: (e, 0, f)),
            pl.BlockSpec((None, d, tf), lambda e, m, f: (e, 0, f)),
            pl.BlockSpec((None, tf, d), lambda e, m, f: (e, f, 0)),
        ],
        out_specs=pl.BlockSpec((None, tmx, d), lambda e, m, f: (e, m, 0)),
        out_shape=jax.ShapeDtypeStruct((ne, rows, d), BF16),
        scratch_shapes=[pltpu.VMEM((tmx, d), F32)],
        compiler_params=pltpu.CompilerParams(
            dimension_semantics=("parallel", "parallel", "arbitrary"), vmem_limit_bytes=_vmem_limit(est)),
        name="experts",
    )(xg, gs, w_gate, w_up, w_down)


def _combine_kernel(pos_ref, og_ref, x1_ref, lg_ref, lb_ref, o_ref, *, cap, alpha):
    ts = x1_ref.shape[0]
    slot = lax.broadcasted_iota(jnp.int32, (cap, ts), 0)
    y = None
    for e in range(og_ref.shape[0]):
        onehot = jnp.where(pos_ref[e] == slot, 1.0, 0.0).astype(BF16)
        part = lax.dot_general(onehot, og_ref[e], (((0,), (0,)), ((), ())), preferred_element_type=F32)
        y = part if y is None else y + part
    o_ref[...] = _layer_norm(alpha * x1_ref[...] + y, lg_ref[...], lb_ref[...])


def _combine(pos4, og, x1, ln_g, ln_b, cap, alpha, ts=256):
    bsz, ne, _, seq = pos4.shape
    t, d = x1.shape
    ts = min(ts, seq)
    nth = seq // ts
    est = 2 * ne * cap * d * 2 + 4 * ts * d * 4 + 6 * ts * d * 4 + 4 * cap * ts * 4
    return pl.pallas_call(
        functools.partial(_combine_kernel, cap=cap, alpha=alpha),
        grid=(bsz, nth),
        in_specs=[
            pl.BlockSpec((None, ne, 1, ts), lambda b, h: (b, 0, 0, h)),
            pl.BlockSpec((ne, cap, d), lambda b, h: (0, b, 0)),
            pl.BlockSpec((ts, d), lambda b, h: (b * nth + h, 0)),
            pl.BlockSpec((1, d), lambda b, h: (0, 0)),
            pl.BlockSpec((1, d), lambda b, h: (0, 0)),
        ],
        out_specs=pl.BlockSpec((ts, d), lambda b, h: (b * nth + h, 0)),
        out_shape=jax.ShapeDtypeStruct((t, d), F32),
        compiler_params=pltpu.CompilerParams(
            dimension_semantics=("parallel", "arbitrary"), vmem_limit_bytes=_vmem_limit(est)),
        name="combine",
    )(pos4, og, x1, ln_g, ln_b)


def kernel(x, w_in, b_gate, a_q_norm, a_k_norm, b_lambda, b_subln, w_a_proj, w_b_proj, w_o, ln1_g, ln1_b,
           w_router, w_gate, w_up, w_down, ln2_g, ln2_b):
    bsz, seq, d = x.shape
    depth = w_in.shape[0]
    ne = w_router.shape[2]
    cap = CAPACITY_FACTOR * seq // ne
    alpha = (2.0 * depth) ** 0.25
    qscale = HEAD_DIM ** -0.5 * LOG2E
    tabs = _rope_tables(seq)
    ones = lambda n: jnp.ones((n,), F32)

    x2d = x.reshape(bsz * seq, d)
    for l in range(depth):
        lam_init = 0.8 - 0.6 * math.exp(-0.3 * l)
        colscale = jnp.concatenate([
            jnp.tile(a_q_norm[l][_PERM_A], A_Q_HEADS) * qscale, jnp.tile(a_k_norm[l][_PERM_A], A_KV_HEADS),
            ones(COL_A_V), ones(COL_B_Q) * qscale, ones(COL_B_K + COL_B_V)])[None, :]
        w_qkv, w_g = _prep_w(w_in[l])
        qkv, xb = _inproj_qkv(x2d, w_qkv, tabs, colscale, seq)
        gates = _inproj_gates(xb, w_g, b_gate[l][None, :])
        oa = _attn_a(qkv, bsz, seq)
        ob = _attn_b(qkv, b_lambda[l], b_subln[l][None, :], bsz, seq, lam_init)
        merged = _merge(oa, ob, gates, w_a_proj[l].astype(BF16), w_b_proj[l].astype(BF16))
        wr_hi = w_router[l].astype(BF16)
        wr_lo = (w_router[l] - wr_hi.astype(F32)).astype(BF16)
        x1, x1b, aff = _outproj(merged, x2d, w_o[l].astype(BF16), ln1_g[l][None, :], ln1_b[l][None, :],
                                jnp.concatenate([wr_hi, wr_lo], axis=1), alpha)
        aff_t = jnp.swapaxes(aff.reshape(bsz, seq, ne), 1, 2)
        pos = _topk(aff_t, cap)
        pos4 = pos.reshape(bsz, ne, 1, seq)
        xg, gs = _gather(pos4, aff_t.reshape(bsz, ne, 1, seq), x1b, cap)
        og = _experts(xg, gs, w_gate[l], w_up[l], w_down[l])
        x2d = _combine(pos4, og, x1, ln2_g[l][None, :], ln2_b[l][None, :], cap, alpha)
    return x2d.reshape(bsz, seq, d)
```

```python
import functools
import math

import numpy as np

import jax
import jax.numpy as jnp
from jax import lax
from jax.experimental import pallas as pl
from jax.experimental.pallas import tpu as pltpu

F32 = jnp.float32
BF16 = jnp.bfloat16

HEAD_DIM = 128
A_Q_HEADS = 8
A_KV_HEADS = 2
A_GROUP = A_Q_HEADS // A_KV_HEADS
A_ROPE_THETA = 10000.0
A_WIDTH = A_Q_HEADS * HEAD_DIM
B_HEADS = 4
B_QK_DIM = 128
B_V_DIM = 2 * B_QK_DIM
B_WIDTH = B_HEADS * B_V_DIM
PARTIAL_ROPE_THETA = 500000.0
PARTIAL_ROPE_DIMS = B_QK_DIM // 4
GRID_W = 64
CAPACITY_FACTOR = 2
RMS_EPS = 1e-6
LN_EPS = 1e-5

COL_A_Q = A_Q_HEADS * HEAD_DIM
COL_A_K = A_KV_HEADS * HEAD_DIM
COL_A_V = A_KV_HEADS * HEAD_DIM
COL_B_Q = B_HEADS * 2 * B_QK_DIM
COL_B_K = B_HEADS * 2 * B_QK_DIM
COL_B_V = B_HEADS * B_V_DIM
QKV_COLS = COL_A_Q + COL_A_K + COL_A_V + COL_B_Q + COL_B_K + COL_B_V
OFF_A_K = COL_A_Q
OFF_A_V = OFF_A_K + COL_A_K
OFF_B_Q = OFF_A_V + COL_A_V
OFF_B_K = OFF_B_Q + COL_B_Q
OFF_B_V = OFF_B_K + COL_B_K

LANES = 128
V7X_VMEM_BYTES = 64 * 1024 * 1024
LOG2E = math.log2(math.e)


def _vmem_limit(estimate_bytes):
    return int(min(estimate_bytes * 5 // 4 + (2 << 20), V7X_VMEM_BYTES - (6 << 20)))


_QUARTER = HEAD_DIM // 4
_PERM_A = np.concatenate([np.arange(0, _QUARTER), np.arange(2 * _QUARTER, 3 * _QUARTER),
                          np.arange(_QUARTER, 2 * _QUARTER), np.arange(3 * _QUARTER, HEAD_DIM)])
_HALF_B = PARTIAL_ROPE_DIMS // 2
_PERM_B = np.arange(B_QK_DIM)
_PERM_B[_HALF_B:2 * _HALF_B] = np.arange(LANES // 2, LANES // 2 + _HALF_B)
_PERM_B[LANES // 2:LANES // 2 + _HALF_B] = np.arange(_HALF_B, 2 * _HALF_B)


def _col_kind(col):
    if col < OFF_A_V:
        return "norm_rope"
    if col < OFF_B_Q:
        return "plain"
    if col < OFF_B_V:
        return "rope"
    if col < QKV_COLS:
        return "plain"
    return "gate"


def _runs(kinds):
    out, start = [], 0
    for i in range(1, len(kinds) + 1):
        if i == len(kinds) or kinds[i] != kinds[start]:
            out.append((start, i, kinds[start]))
            start = i
    return out


def _swap_lane_blocks(w, lo, hi, width):
    lane = lax.broadcasted_iota(jnp.int32, w.shape, 1)
    up = pltpu.roll(w, LANES - (hi - lo), 1)
    down = pltpu.roll(w, hi - lo, 1)
    w = jnp.where((lane >= lo) & (lane < lo + width), up, w)
    return jnp.where((lane >= hi) & (lane < hi + width), down, w)


def _prep_w_kernel(w_ref, oq_ref, og_ref, *, tn, n_qkv, n_steps):
    j = pl.program_id(0)
    kinds = [tuple(_col_kind(s * tn + h * LANES) for h in range(tn // LANES)) for s in range(n_steps)]
    for start, stop, ks in _runs(kinds):
        @pl.when((j >= start) & (j < stop))
        def _(ks=ks, o_ref=oq_ref if start < n_qkv else og_ref):
            for h, kind in enumerate(ks):
                sl = slice(h * LANES, (h + 1) * LANES)
                w = w_ref[:, sl]
                if kind == "norm_rope":
                    w = _swap_lane_blocks(w, _QUARTER, 2 * _QUARTER, _QUARTER)
                elif kind == "rope":
                    w = _swap_lane_blocks(w, _HALF_B, LANES // 2, _HALF_B)
                o_ref[:, sl] = w.astype(BF16)


def _prep_w(w, tn=512):
    d, cols = w.shape
    n_qkv = QKV_COLS // tn
    n_g = (cols - QKV_COLS) // tn
    return pl.pallas_call(
        functools.partial(_prep_w_kernel, tn=tn, n_qkv=n_qkv, n_steps=n_qkv + n_g),
        grid=(n_qkv + n_g,),
        in_specs=[pl.BlockSpec((d, tn), lambda j: (0, j))],
        out_specs=[
            pl.BlockSpec((d, tn), lambda j: (0, jnp.minimum(j, n_qkv - 1))),
            pl.BlockSpec((d, tn), lambda j: (0, jnp.clip(j - n_qkv, 0, n_g - 1))),
        ],
        out_shape=[jax.ShapeDtypeStruct((d, QKV_COLS), BF16), jax.ShapeDtypeStruct((d, cols - QKV_COLS), BF16)],
        compiler_params=pltpu.CompilerParams(
            dimension_semantics=("arbitrary",), vmem_limit_bytes=_vmem_limit(2 * d * tn * 8 + 2 * d * tn * 4)),
        name="prep_w",
    )(w)


def _rope_tables(seq):
    pos = jnp.arange(seq)

    def cos_sin(p, dim, theta):
        inv = theta ** (-jnp.arange(0, dim, 2, dtype=F32) / dim)
        ang = p.astype(F32)[:, None] * inv[None, :]
        return jnp.cos(ang), jnp.sin(ang)

    cr, sr = cos_sin(pos // GRID_W, HEAD_DIM // 2, A_ROPE_THETA)
    cc, sc = cos_sin(pos % GRID_W, HEAD_DIM // 2, A_ROPE_THETA)
    cl, sl = cos_sin(pos, PARTIAL_ROPE_DIMS, PARTIAL_ROPE_THETA)
    one = jnp.ones((seq, LANES // 2 - _HALF_B), F32)
    zero = jnp.zeros((seq, LANES // 2 - _HALF_B), F32)
    cos_a = jnp.concatenate([cr, cc, cr, cc], axis=-1)
    sin_a = jnp.concatenate([-sr, -sc, sr, sc], axis=-1)
    cos_b = jnp.concatenate([cl, one, cl, one], axis=-1)
    sin_b = jnp.concatenate([-sl, zero, sl, zero], axis=-1)
    return cos_a, sin_a, cos_b, sin_b


def _inproj_qkv_kernel(x_ref, w_ref, ca_ref, sa_ref, cb_ref, sb_ref, cs_ref, qkv_ref, xb_ref, *, tn, tc, n_steps):
    j = pl.program_id(1)

    @pl.when(j == 0)
    def _():
        xb_ref[...] = x_ref[...].astype(BF16)

    def epilogue(c, kind, y):
        cols = slice(c * tc, (c + 1) * tc)
        if kind == "plain":
            qkv_ref[:, cols] = y.astype(BF16)
            return
        if kind == "norm_rope":
            head = lambda ax: lax.broadcasted_iota(jnp.int32, (tc, tc), ax) // HEAD_DIM
            blockdiag = jnp.where(head(0) == head(1), 1.0 / HEAD_DIM, 0.0).astype(BF16)
            ms = jnp.dot((y * y).astype(BF16), blockdiag, preferred_element_type=F32)
            z = y * lax.rsqrt(ms + RMS_EPS) * cs_ref[:, cols]
            cos_ref, sin_ref = ca_ref, sa_ref
        else:
            z = y * cs_ref[:, cols]
            cos_ref, sin_ref = cb_ref, sb_ref
        for h in range(tc // LANES):
            zh = z[:, h * LANES:(h + 1) * LANES]
            out = zh * cos_ref[...] + pltpu.roll(zh, LANES // 2, 1) * sin_ref[...]
            qkv_ref[:, c * tc + h * LANES:c * tc + (h + 1) * LANES] = out.astype(BF16)

    kinds = [tuple(_col_kind(s * tn + c * tc) for c in range(tn // tc)) for s in range(n_steps)]
    for start, stop, ks in _runs(kinds):
        @pl.when((j >= start) & (j < stop))
        def _(ks=ks):
            ys = [jnp.dot(xb_ref[...], w_ref[:, c * tc:(c + 1) * tc], preferred_element_type=F32)
                  for c in range(len(ks))]
            for c, kind in enumerate(ks):
                epilogue(c, kind, ys[c])


def _inproj_qkv(x2d, w_qkv, tabs, colscale, seq, tm=512, tn=1536, tc=256):
    t, d = x2d.shape
    tm = min(tm, seq)
    n_steps = QKV_COLS // tn
    per_seq = seq // tm
    tab_spec = pl.BlockSpec((tm, LANES), lambda i, j: (i % per_seq, 0))
    est = (2 * tm * d * 4 + 2 * tm * d * 2 + 2 * d * tn * 2 + 8 * tm * LANES * 4 + 2 * tm * tn * 2
           + 3 * tm * tn * 4)
    return pl.pallas_call(
        functools.partial(_inproj_qkv_kernel, tn=tn, tc=tc, n_steps=n_steps),
        grid=(t // tm, n_steps),
        in_specs=[
            pl.BlockSpec((tm, d), lambda i, j: (i, 0)),
            pl.BlockSpec((d, tn), lambda i, j: (0, j)),
            tab_spec, tab_spec, tab_spec, tab_spec,
            pl.BlockSpec((1, tn), lambda i, j: (0, j)),
        ],
        out_specs=[
            pl.BlockSpec((tm, tn), lambda i, j: (i, j)),
            pl.BlockSpec((tm, d), lambda i, j: (i, 0)),
        ],
        out_shape=[jax.ShapeDtypeStruct((t, QKV_COLS), BF16), jax.ShapeDtypeStruct((t, d), BF16)],
        compiler_params=pltpu.CompilerParams(
            dimension_semantics=("arbitrary", "arbitrary"), vmem_limit_bytes=_vmem_limit(est)),
        name="inproj_qkv",
    )(x2d, w_qkv, *tabs, colscale)


def _inproj_gates_kernel(x_ref, w_ref, bias_ref, g_ref, *, tc):
    n = w_ref.shape[1] // tc
    ys = [jnp.dot(x_ref[...], w_ref[:, c * tc:(c + 1) * tc], preferred_element_type=F32) for c in range(n)]
    for c in range(n):
        cols = slice(c * tc, (c + 1) * tc)
        g_ref[:, cols] = (0.5 * jnp.tanh(0.5 * (ys[c] + bias_ref[:, cols])) + 0.5).astype(BF16)


def _inproj_gates(xb, w_gate, bias, tm=2048, tn=1024, tc=256):
    t, d = xb.shape
    cols = w_gate.shape[1]
    tm = min(tm, t)
    tn = min(tn, cols)
    est = 2 * tm * d * 2 + 2 * d * tn * 2 + 2 * tm * tn * 2 + 3 * tm * tn * 4
    return pl.pallas_call(
        functools.partial(_inproj_gates_kernel, tc=tc),
        grid=(t // tm, cols // tn),
        in_specs=[
            pl.BlockSpec((tm, d), lambda i, j: (i, 0)),
            pl.BlockSpec((d, tn), lambda i, j: (0, j)),
            pl.BlockSpec((1, tn), lambda i, j: (0, j)),
        ],
        out_specs=pl.BlockSpec((tm, tn), lambda i, j: (i, j)),
        out_shape=jax.ShapeDtypeStruct((t, cols), BF16),
        compiler_params=pltpu.CompilerParams(
            dimension_semantics=("parallel", "arbitrary"), vmem_limit_bytes=_vmem_limit(est)),
        name="inproj_gates",
    )(xb, w_gate, bias)


def _softmax_parts(q, k):
    s = lax.dot_general(q, k, (((1,), (1,)), ((), ())), preferred_element_type=F32)
    p = jnp.exp2(s - jnp.max(s, axis=-1, keepdims=True))
    return p, jnp.sum(p, axis=-1, keepdims=True)


def _attn_a_kernel(q_ref, k_ref, v_ref, o_ref, *, tq, unroll):
    nq = q_ref.shape[0] // tq

    def body(qi, carry):
        r = pl.multiple_of(qi * tq, tq)
        for g in range(A_GROUP):
            sl = slice(g * HEAD_DIM, (g + 1) * HEAD_DIM)
            p, l = _softmax_parts(q_ref[pl.ds(r, tq), sl], k_ref[...])
            o = jnp.dot(p.astype(BF16), v_ref[...], preferred_element_type=F32)
            o_ref[pl.ds(r, tq), sl] = (o / l).astype(BF16)
        return carry

    lax.fori_loop(0, nq, body, 0, unroll=unroll)


def _attn_a(qkv, bsz, seq, tq=256, unroll=4):
    t = qkv.shape[0]
    gw = A_GROUP * HEAD_DIM
    tq = min(tq, seq)
    unroll = min(unroll, seq // tq)
    est = 2 * (2 * seq * gw * 2 + 2 * seq * HEAD_DIM * 2) + 4 * unroll * tq * seq * 4
    return pl.pallas_call(
        functools.partial(_attn_a_kernel, tq=tq, unroll=unroll),
        grid=(bsz, A_KV_HEADS),
        in_specs=[
            pl.BlockSpec((seq, gw), lambda b, h: (b, h)),
            pl.BlockSpec((seq, HEAD_DIM), lambda b, h: (b, OFF_A_K // HEAD_DIM + h)),
            pl.BlockSpec((seq, HEAD_DIM), lambda b, h: (b, OFF_A_V // HEAD_DIM + h)),
        ],
        out_specs=pl.BlockSpec((seq, gw), lambda b, h: (b, h)),
        out_shape=jax.ShapeDtypeStruct((t, A_WIDTH), BF16),
        compiler_params=pltpu.CompilerParams(
            dimension_semantics=("parallel", "parallel"), vmem_limit_bytes=_vmem_limit(est)),
        name="attn_a",
    )(qkv, qkv, qkv)


def _attn_b_kernel(lam_ref, sub_ref, q_ref, k_ref, v_ref, o_ref, *, tq, lam_init, unroll):
    nq = q_ref.shape[0] // tq
    lp = lam_ref[...]
    lam = (jnp.exp(jnp.sum(lp[0:1] * lp[1:2], axis=-1, keepdims=True))
           - jnp.exp(jnp.sum(lp[2:3] * lp[3:4], axis=-1, keepdims=True)) + lam_init)
    d = B_QK_DIM

    def body(qi, carry):
        r = pl.multiple_of(qi * tq, tq)
        p1, l1 = _softmax_parts(q_ref[pl.ds(r, tq), 0:d], k_ref[:, 0:d])
        p2, l2 = _softmax_parts(q_ref[pl.ds(r, tq), d:2 * d], k_ref[:, d:2 * d])
        o1 = jnp.dot(p1.astype(BF16), v_ref[...], preferred_element_type=F32)
        o2 = jnp.dot(p2.astype(BF16), v_ref[...], preferred_element_type=F32)
        o = o1 * (1.0 / l1) - o2 * (lam / l2)
        ms = jnp.mean(o * o, axis=-1, keepdims=True)
        o = o * lax.rsqrt(ms + RMS_EPS) * sub_ref[...] * (1.0 - lam_init)
        o_ref[pl.ds(r, tq), :] = o.astype(BF16)
        return carry

    lax.fori_loop(0, nq, body, 0, unroll=unroll)


def _attn_b(qkv, lam_p, subln, bsz, seq, lam_init, tq=256, unroll=8):
    t = qkv.shape[0]
    tq = min(tq, seq)
    unroll = min(unroll, seq // tq)
    w = 2 * B_QK_DIM
    est = 2 * 4 * seq * w * 2 + 5 * unroll * tq * seq * 4
    return pl.pallas_call(
        functools.partial(_attn_b_kernel, tq=tq, lam_init=lam_init, unroll=unroll),
        grid=(bsz, B_HEADS),
        in_specs=[
            pl.BlockSpec((4, B_QK_DIM), lambda b, h: (0, 0)),
            pl.BlockSpec((1, B_V_DIM), lambda b, h: (0, 0)),
            pl.BlockSpec((seq, w), lambda b, h: (b, OFF_B_Q // w + h)),
            pl.BlockSpec((seq, w), lambda b, h: (b, OFF_B_K // w + h)),
            pl.BlockSpec((seq, B_V_DIM), lambda b, h: (b, OFF_B_V // B_V_DIM + h)),
        ],
        out_specs=pl.BlockSpec((seq, B_V_DIM), lambda b, h: (b, h)),
        out_shape=jax.ShapeDtypeStruct((t, B_WIDTH), BF16),
        compiler_params=pltpu.CompilerParams(
            dimension_semantics=("parallel", "parallel"), vmem_limit_bytes=_vmem_limit(est)),
        name="attn_b",
    )(lam_p, subln, qkv, qkv, qkv)


def _layer_norm(v, g, b):
    mu = jnp.mean(v, axis=-1, keepdims=True)
    c = v - mu
    var = jnp.mean(c * c, axis=-1, keepdims=True)
    return c * lax.rsqrt(var + LN_EPS) * g + b


def _merge_kernel(oa_ref, ob_ref, g0_ref, g1_ref, wa_ref, wb_ref, m_ref):
    ya = jnp.dot(oa_ref[...], wa_ref[...], preferred_element_type=F32)
    yb = jnp.dot(ob_ref[...], wb_ref[...], preferred_element_type=F32)
    m_ref[...] = (g0_ref[...].astype(F32) * ya + g1_ref[...].astype(F32) * yb).astype(BF16)


def _merge(oa, ob, g, wa, wb, tm=1024, tn=1024):
    t = oa.shape[0]
    d = wa.shape[1]
    tm = min(tm, t)
    tn = min(tn, d)
    nn = d // tn
    est = (2 * 2 * tm * A_WIDTH * 2 + 2 * 2 * A_WIDTH * tn * 2 + 2 * 2 * tm * tn * 2 + 2 * tm * tn * 2
           + 4 * tm * tn * 4)
    return pl.pallas_call(
        _merge_kernel,
        grid=(t // tm, nn),
        in_specs=[
            pl.BlockSpec((tm, A_WIDTH), lambda i, n: (i, 0)),
            pl.BlockSpec((tm, B_WIDTH), lambda i, n: (i, 0)),
            pl.BlockSpec((tm, tn), lambda i, n: (i, n)),
            pl.BlockSpec((tm, tn), lambda i, n: (i, nn + n)),
            pl.BlockSpec((A_WIDTH, tn), lambda i, n: (0, n)),
            pl.BlockSpec((B_WIDTH, tn), lambda i, n: (0, n)),
        ],
        out_specs=pl.BlockSpec((tm, tn), lambda i, n: (i, n)),
        out_shape=jax.ShapeDtypeStruct((t, d), BF16),
        compiler_params=pltpu.CompilerParams(
            dimension_semantics=("parallel", "arbitrary"), vmem_limit_bytes=_vmem_limit(est)),
        name="merge",
    )(oa, ob, g, g, wa, wb)


def _outproj_kernel(m_ref, x_ref, wo_ref, lg_ref, lb_ref, wr_ref, x1_ref, x1b_ref, aff_ref, *, alpha):
    ne = aff_ref.shape[1]
    mix = jnp.dot(m_ref[...], wo_ref[...], preferred_element_type=F32)
    x1 = _layer_norm(alpha * x_ref[...] + mix, lg_ref[...], lb_ref[...])
    x1_ref[...] = x1
    hi = x1.astype(BF16)
    x1b_ref[...] = hi
    lo = (x1 - hi.astype(F32)).astype(BF16)
    both = jnp.dot(hi, wr_ref[...], preferred_element_type=F32)
    low = jnp.dot(lo, wr_ref[:, 0:ne], preferred_element_type=F32)
    logits = both[:, 0:ne] + (both[:, ne:2 * ne] + low)
    e = jnp.exp(logits - jnp.max(logits, axis=-1, keepdims=True))
    aff_ref[...] = e / jnp.sum(e, axis=-1, keepdims=True)


def _outproj(merged, x2d, wo, ln_g, ln_b, wr2, alpha, tm=512):
    t, d = x2d.shape
    ne = wr2.shape[1] // 2
    tm = min(tm, t)
    const = lambda shape: pl.BlockSpec(shape, lambda i: (0, 0), pipeline_mode=pl.Buffered(1))
    row = lambda width: pl.BlockSpec((tm, width), lambda i: (i, 0))
    est = d * d * 2 + 2 * tm * d * 2 + 2 * tm * d * 4 + 2 * tm * d * 6 + 5 * tm * d * 4 + d * LANES * 2
    return pl.pallas_call(
        functools.partial(_outproj_kernel, alpha=alpha),
        grid=(t // tm,),
        in_specs=[row(d), row(d), const((d, d)), const((1, d)), const((1, d)), const((d, 2 * ne))],
        out_specs=[row(d), row(d), row(ne)],
        out_shape=[
            jax.ShapeDtypeStruct((t, d), F32),
            jax.ShapeDtypeStruct((t, d), BF16),
            jax.ShapeDtypeStruct((t, ne), F32),
        ],
        compiler_params=pltpu.CompilerParams(
            dimension_semantics=("parallel",), vmem_limit_bytes=_vmem_limit(est)),
        name="outproj",
    )(merged, x2d, wo, ln_g, ln_b, wr2)


def _exclusive_prefix_chunks(mask_chunks, tri):
    out = []
    offset = None
    for m in mask_chunks:
        mf = m.astype(F32)
        incl = jnp.dot(m.astype(BF16), tri, preferred_element_type=F32)
        excl = incl - mf
        out.append(excl if offset is None else excl + offset)
        total = incl[:, LANES - 1:LANES]
        offset = total if offset is None else offset + total
    return out


def _topk_kernel(aff_ref, pos_ref, *, cap):
    ne, seq = aff_ref.shape
    aff = aff_ref[...]

    def enough(cand):
        cnt = jnp.sum((aff >= pltpu.bitcast(cand, F32)).astype(jnp.int32), axis=-1, keepdims=True)
        return cnt >= cap

    def search(i, thr):
        hi = lax.shift_left(jnp.int32(1), 30 - 2 * i)
        lo = lax.shift_left(jnp.int32(1), 29 - 2 * i)
        c3, c2, c1 = thr | hi | lo, thr | hi, thr | lo
        return jnp.where(enough(c3), c3, jnp.where(enough(c2), c2, jnp.where(enough(c1), c1, thr)))

    thr = lax.fori_loop(0, 15, search, jnp.zeros((ne, 1), jnp.int32))
    thr = jnp.where(enough(thr | 1), thr | 1, thr)
    gt = aff >= pltpu.bitcast(thr + 1, F32)
    eq = (aff >= pltpu.bitcast(thr, F32)) & jnp.logical_not(gt)
    need = cap - jnp.sum(gt.astype(jnp.int32), axis=-1, keepdims=True)
    tri = (lax.broadcasted_iota(jnp.int32, (LANES, LANES), 0)
           <= lax.broadcasted_iota(jnp.int32, (LANES, LANES), 1)).astype(BF16)
    chunks = [slice(c * LANES, (c + 1) * LANES) for c in range(seq // LANES)]
    eq_rank = _exclusive_prefix_chunks([eq[:, c] for c in chunks], tri)
    sel = [gt[:, c] | (eq[:, c] & (r < need.astype(F32))) for c, r in zip(chunks, eq_rank)]
    slot = _exclusive_prefix_chunks(sel, tri)
    for c, s, p in zip(chunks, sel, slot):
        pos_ref[:, c] = jnp.where(s, p.astype(jnp.int32), -1)


def _topk(aff_t, cap):
    bsz, ne, seq = aff_t.shape
    return pl.pallas_call(
        functools.partial(_topk_kernel, cap=cap),
        grid=(bsz,),
        in_specs=[pl.BlockSpec((None, ne, seq), lambda b: (b, 0, 0))],
        out_specs=pl.BlockSpec((None, ne, seq), lambda b: (b, 0, 0)),
        out_shape=jax.ShapeDtypeStruct((bsz, ne, seq), jnp.int32),
        compiler_params=pltpu.CompilerParams(dimension_semantics=("parallel",)),
        name="topk",
    )(aff_t)


def _gather_kernel(pos_ref, aff_ref, x_ref, xg_ref, gs_ref, *, cap):
    seq = x_ref.shape[0]
    slot = lax.broadcasted_iota(jnp.int32, (cap, seq), 0)
    for e in range(pos_ref.shape[0]):
        hit = pos_ref[e] == slot
        xg = jnp.dot(jnp.where(hit, 1.0, 0.0).astype(BF16), x_ref[...], preferred_element_type=F32)
        xg_ref[e] = xg.astype(BF16)
        gs_ref[e] = jnp.sum(jnp.where(hit, aff_ref[e], 0.0), axis=-1, keepdims=True)


def _gather(pos4, aff4, x1b, cap, eg=8):
    bsz, ne, _, seq = pos4.shape
    d = x1b.shape[1]
    eg = min(eg, ne)
    row_spec = pl.BlockSpec((None, eg, 1, seq), lambda b, e: (b, e, 0, 0))
    est = 2 * seq * d * 2 + 2 * eg * cap * d * 2 + 4 * cap * seq * 4 + 2 * cap * d * 4
    return pl.pallas_call(
        functools.partial(_gather_kernel, cap=cap),
        grid=(bsz, ne // eg),
        in_specs=[row_spec, row_spec, pl.BlockSpec((seq, d), lambda b, e: (b, 0))],
        out_specs=[
            pl.BlockSpec((eg, cap, d), lambda b, e: (e, b, 0)),
            pl.BlockSpec((eg, cap, 1), lambda b, e: (e, b, 0)),
        ],
        out_shape=[
            jax.ShapeDtypeStruct((ne, bsz * cap, d), BF16),
            jax.ShapeDtypeStruct((ne, bsz * cap, 1), F32),
        ],
        compiler_params=pltpu.CompilerParams(
            dimension_semantics=("parallel", "arbitrary"), vmem_limit_bytes=_vmem_limit(est)),
        name="gather",
    )(pos4, aff4, x1b)


def _experts_kernel(xg_ref, gs_ref, wg_ref, wu_ref, wd_ref, og_ref, acc_ref):
    f = pl.program_id(2)

    x = xg_ref[...]
    hg = jnp.dot(x, wg_ref[...].astype(BF16), preferred_element_type=F32)
    hu = jnp.dot(x, wu_ref[...].astype(BF16), preferred_element_type=F32)
    hid = hg * (0.5 * jnp.tanh(0.5 * hg) + 0.5) * hu
    down = jnp.dot(hid.astype(BF16), wd_ref[...].astype(BF16), preferred_element_type=F32)
    acc_ref[...] = jnp.where(f == 0, 0.0, acc_ref[...]) + down

    @pl.when(f == pl.num_programs(2) - 1)
    def _():
        og_ref[...] = (acc_ref[...] * gs_ref[...]).astype(BF16)


def _experts(xg, gs, w_gate, w_up, w_down, tmx=1024, tf=512):
    ne, rows, d = xg.shape
    ff = w_gate.shape[2]
    tmx = min(tmx, rows)
    tf = min(tf, ff)
    est = (2 * tmx * d * 2 + 2 * tmx * LANES * 4 + 2 * 3 * d * tf * 4 + 2 * tmx * d * 2 + tmx * d * 4
           + 3 * d * tf * 2 + 4 * tmx * tf * 4 + tmx * d * 4)
    return pl.pallas_call(
        _experts_kernel,
        grid=(ne, rows // tmx, ff // tf),
        in_specs=[
            pl.BlockSpec((None, tmx, d), lambda e, m, f: (e, m, 0)),
            pl.BlockSpec((None, tmx, 1), lambda e, m, f: (e, m, 0)),
            pl.BlockSpec((None, d, tf), lambda e, m, f: (e, 0, f)),
            pl.BlockSpec((None, d, tf), lambda e, m, f: (e, 0, f)),
            pl.BlockSpec((None, tf, d), lambda e, m, f: (e, f, 0)),
        ],
        out_specs=pl.BlockSpec((None, tmx, d), lambda e, m, f: (e, m, 0)),
        out_shape=jax.ShapeDtypeStruct((ne, rows, d), BF16),
        scratch_shapes=[pltpu.VMEM((tmx, d), F32)],
        compiler_params=pltpu.CompilerParams(
            dimension_semantics=("parallel", "parallel", "arbitrary"), vmem_limit_bytes=_vmem_limit(est)),
        name="experts",
    )(xg, gs, w_gate, w_up, w_down)


def _combine_kernel(pos_ref, og_ref, x1_ref, lg_ref, lb_ref, o_ref, *, cap, alpha):
    ts = x1_ref.shape[0]
    slot = lax.broadcasted_iota(jnp.int32, (cap, ts), 0)
    y = None
    for e in range(og_ref.shape[0]):
        onehot = jnp.where(pos_ref[e] == slot, 1.0, 0.0).astype(BF16)
        part = lax.dot_general(onehot, og_ref[e], (((0,), (0,)), ((), ())), preferred_element_type=F32)
        y = part if y is None else y + part
    o_ref[...] = _layer_norm(alpha * x1_ref[...] + y, lg_ref[...], lb_ref[...])


def _combine(pos4, og, x1, ln_g, ln_b, cap, alpha, ts=256):
    bsz, ne, _, seq = pos4.shape
    t, d = x1.shape
    ts = min(ts, seq)
    nth = seq // ts
    est = 2 * ne * cap * d * 2 + 4 * ts * d * 4 + 6 * ts * d * 4 + 4 * cap * ts * 4
    return pl.pallas_call(
        functools.partial(_combine_kernel, cap=cap, alpha=alpha),
        grid=(bsz, nth),
        in_specs=[
            pl.BlockSpec((None, ne, 1, ts), lambda b, h: (b, 0, 0, h)),
            pl.BlockSpec((ne, cap, d), lambda b, h: (0, b, 0)),
            pl.BlockSpec((ts, d), lambda b, h: (b * nth + h, 0)),
            pl.BlockSpec((1, d), lambda b, h: (0, 0)),
            pl.BlockSpec((1, d), lambda b, h: (0, 0)),
        ],
        out_specs=pl.BlockSpec((ts, d), lambda b, h: (b * nth + h, 0)),
        out_shape=jax.ShapeDtypeStruct((t, d), F32),
        compiler_params=pltpu.CompilerParams(
            dimension_semantics=("parallel", "arbitrary"), vmem_limit_bytes=_vmem_limit(est)),
        name="combine",
    )(pos4, og, x1, ln_g, ln_b)


def kernel(x, w_in, b_gate, a_q_norm, a_k_norm, b_lambda, b_subln, w_a_proj, w_b_proj, w_o, ln1_g, ln1_b,
           w_router, w_gate, w_up, w_down, ln2_g, ln2_b):
    bsz, seq, d = x.shape
    depth = w_in.shape[0]
    ne = w_router.shape[2]
    cap = CAPACITY_FACTOR * seq // ne
    alpha = (2.0 * depth) ** 0.25
    qscale = HEAD_DIM ** -0.5 * LOG2E
    tabs = _rope_tables(seq)
    ones = lambda n: jnp.ones((n,), F32)

    x2d = x.reshape(bsz * seq, d)
    for l in range(depth):
        lam_init = 0.8 - 0.6 * math.exp(-0.3 * l)
        colscale = jnp.concatenate([
            jnp.tile(a_q_norm[l][_PERM_A], A_Q_HEADS) * qscale, jnp.tile(a_k_norm[l][_PERM_A], A_KV_HEADS),
            ones(COL_A_V), ones(COL_B_Q) * qscale, ones(COL_B_K + COL_B_V)])[None, :]
        w_qkv, w_g = _prep_w(w_in[l])
        qkv, xb = _inproj_qkv(x2d, w_qkv, tabs, colscale, seq)
        gates = _inproj_gates(xb, w_g, b_gate[l][None, :])
        oa = _attn_a(qkv, bsz, seq)
        ob = _attn_b(qkv, b_lambda[l], b_subln[l][None, :], bsz, seq, lam_init)
        merged = _merge(oa, ob, gates, w_a_proj[l].astype(BF16), w_b_proj[l].astype(BF16))
        wr_hi = w_router[l].astype(BF16)
        wr_lo = (w_router[l] - wr_hi.astype(F32)).astype(BF16)
        x1, x1b, aff = _outproj(merged, x2d, w_o[l].astype(BF16), ln1_g[l][None, :], ln1_b[l][None, :],
                                jnp.concatenate([wr_hi, wr_lo], axis=1), alpha)
        aff_t = jnp.swapaxes(aff.reshape(bsz, seq, ne), 1, 2)
        pos = _topk(aff_t, cap)
        pos4 = pos.reshape(bsz, ne, 1, seq)
        xg, gs = _gather(pos4, aff_t.reshape(bsz, ne, 1, seq), x1b, cap)
        og = _experts(xg, gs, w_gate[l], w_up[l], w_down[l])
        x2d = _combine(pos4, og, x1, ln2_g[l][None, :], ln2_b[l][None, :], cap, alpha)
    return x2d.reshape(bsz, seq, d)
```

```python
import functools
import math

import numpy as np

import jax
import jax.numpy as jnp
from jax import lax
from jax.experimental import pallas as pl
from jax.experimental.pallas import tpu as pltpu

F32 = jnp.float32
BF16 = jnp.bfloat16

HEAD_DIM = 128
A_Q_HEADS = 8
A_KV_HEADS = 2
A_GROUP = A_Q_HEADS // A_KV_HEADS
A_ROPE_THETA = 10000.0
A_WIDTH = A_Q_HEADS * HEAD_DIM
B_HEADS = 4
B_QK_DIM = 128
B_V_DIM = 2 * B_QK_DIM
B_WIDTH = B_HEADS * B_V_DIM
PARTIAL_ROPE_THETA = 500000.0
PARTIAL_ROPE_DIMS = B_QK_DIM // 4
GRID_W = 64
CAPACITY_FACTOR = 2
RMS_EPS = 1e-6
LN_EPS = 1e-5

COL_A_Q = A_Q_HEADS * HEAD_DIM
COL_A_K = A_KV_HEADS * HEAD_DIM
COL_A_V = A_KV_HEADS * HEAD_DIM
COL_B_Q = B_HEADS * 2 * B_QK_DIM
COL_B_K = B_HEADS * 2 * B_QK_DIM
COL_B_V = B_HEADS * B_V_DIM
QKV_COLS = COL_A_Q + COL_A_K + COL_A_V + COL_B_Q + COL_B_K + COL_B_V
OFF_A_K = COL_A_Q
OFF_A_V = OFF_A_K + COL_A_K
OFF_B_Q = OFF_A_V + COL_A_V
OFF_B_K = OFF_B_Q + COL_B_Q
OFF_B_V = OFF_B_K + COL_B_K

LANES = 128
V7X_VMEM_BYTES = 64 * 1024 * 1024
LOG2E = math.log2(math.e)


def _vmem_limit(estimate_bytes):
    return int(min(estimate_bytes * 5 // 4 + (2 << 20), V7X_VMEM_BYTES - (6 << 20)))


_QUARTER = HEAD_DIM // 4
_PERM_A = np.concatenate([np.arange(0, _QUARTER), np.arange(2 * _QUARTER, 3 * _QUARTER),
                          np.arange(_QUARTER, 2 * _QUARTER), np.arange(3 * _QUARTER, HEAD_DIM)])
_HALF_B = PARTIAL_ROPE_DIMS // 2
_PERM_B = np.arange(B_QK_DIM)
_PERM_B[_HALF_B:2 * _HALF_B] = np.arange(LANES // 2, LANES // 2 + _HALF_B)
_PERM_B[LANES // 2:LANES // 2 + _HALF_B] = np.arange(_HALF_B, 2 * _HALF_B)


def _col_kind(col):
    if col < OFF_A_V:
        return "norm_rope"
    if col < OFF_B_Q:
        return "plain"
    if col < OFF_B_V:
        return "rope"
    if col < QKV_COLS:
        return "plain"
    return "gate"


def _runs(kinds):
    out, start = [], 0
    for i in range(1, len(kinds) + 1):
        if i == len(kinds) or kinds[i] != kinds[start]:
            out.append((start, i, kinds[start]))
            start = i
    return out


def _swap_lane_blocks(w, lo, hi, width):
    lane = lax.broadcasted_iota(jnp.int32, w.shape, 1)
    up = pltpu.roll(w, LANES - (hi - lo), 1)
    down = pltpu.roll(w, hi - lo, 1)
    w = jnp.where((lane >= lo) & (lane < lo + width), up, w)
    return jnp.where((lane >= hi) & (lane < hi + width), down, w)


def _prep_w_kernel(w_ref, oq_ref, og_ref, *, tn, n_qkv, n_steps):
    j = pl.program_id(0)
    kinds = [tuple(_col_kind(s * tn + h * LANES) for h in range(tn // LANES)) for s in range(n_steps)]
    for start, stop, ks in _runs(kinds):
        @pl.when((j >= start) & (j < stop))
        def _(ks=ks, o_ref=oq_ref if start < n_qkv else og_ref):
            for h, kind in enumerate(ks):
                sl = slice(h * LANES, (h + 1) * LANES)
                w = w_ref[:, sl]
                if kind == "norm_rope":
                    w = _swap_lane_blocks(w, _QUARTER, 2 * _QUARTER, _QUARTER)
                elif kind == "rope":
                    w = _swap_lane_blocks(w, _HALF_B, LANES // 2, _HALF_B)
                o_ref[:, sl] = w.astype(BF16)


def _prep_w(w, tn=512):
    d, cols = w.shape
    n_qkv = QKV_COLS // tn
    n_g = (cols - QKV_COLS) // tn
    return pl.pallas_call(
        functools.partial(_prep_w_kernel, tn=tn, n_qkv=n_qkv, n_steps=n_qkv + n_g),
        grid=(n_qkv + n_g,),
        in_specs=[pl.BlockSpec((d, tn), lambda j: (0, j))],
        out_specs=[
            pl.BlockSpec((d, tn), lambda j: (0, jnp.minimum(j, n_qkv - 1))),
            pl.BlockSpec((d, tn), lambda j: (0, jnp.clip(j - n_qkv, 0, n_g - 1))),
        ],
        out_shape=[jax.ShapeDtypeStruct((d, QKV_COLS), BF16), jax.ShapeDtypeStruct((d, cols - QKV_COLS), BF16)],
        compiler_params=pltpu.CompilerParams(
            dimension_semantics=("arbitrary",), vmem_limit_bytes=_vmem_limit(2 * d * tn * 8 + 2 * d * tn * 4)),
        name="prep_w",
    )(w)


def _rope_tables(seq):
    pos = jnp.arange(seq)

    def cos_sin(p, dim, theta):
        inv = theta ** (-jnp.arange(0, dim, 2, dtype=F32) / dim)
        ang = p.astype(F32)[:, None] * inv[None, :]
        return jnp.cos(ang), jnp.sin(ang)

    cr, sr = cos_sin(pos // GRID_W, HEAD_DIM // 2, A_ROPE_THETA)
    cc, sc = cos_sin(pos % GRID_W, HEAD_DIM // 2, A_ROPE_THETA)
    cl, sl = cos_sin(pos, PARTIAL_ROPE_DIMS, PARTIAL_ROPE_THETA)
    one = jnp.ones((seq, LANES // 2 - _HALF_B), F32)
    zero = jnp.zeros((seq, LANES // 2 - _HALF_B), F32)
    cos_a = jnp.concatenate([cr, cc, cr, cc], axis=-1)
    sin_a = jnp.concatenate([-sr, -sc, sr, sc], axis=-1)
    cos_b = jnp.concatenate([cl, one, cl, one], axis=-1)
    sin_b = jnp.concatenate([-sl, zero, sl, zero], axis=-1)
    return cos_a, sin_a, cos_b, sin_b


def _inproj_qkv_kernel(x_ref, w_ref, ca_ref, sa_ref, cb_ref, sb_ref, cs_ref, qkv_ref, xb_ref, *, tn, tc, n_steps):
    j = pl.program_id(1)

    @pl.when(j == 0)
    def _():
        xb_ref[...] = x_ref[...].astype(BF16)

    def epilogue(c, kind, y):
        cols = slice(c * tc, (c + 1) * tc)
        if kind == "plain":
            qkv_ref[:, cols] = y.astype(BF16)
            return
        if kind == "norm_rope":
            head = lambda ax: lax.broadcasted_iota(jnp.int32, (tc, tc), ax) // HEAD_DIM
            blockdiag = jnp.where(head(0) == head(1), 1.0 / HEAD_DIM, 0.0).astype(BF16)
            ms = jnp.dot((y * y).astype(BF16), blockdiag, preferred_element_type=F32)
            z = y * lax.rsqrt(ms + RMS_EPS) * cs_ref[:, cols]
            cos_ref, sin_ref = ca_ref, sa_ref
        else:
            z = y * cs_ref[:, cols]
            cos_ref, sin_ref = cb_ref, sb_ref
        for h in range(tc // LANES):
            zh = z[:, h * LANES:(h + 1) * LANES]
            out = zh * cos_ref[...] + pltpu.roll(zh, LANES // 2, 1) * sin_ref[...]
            qkv_ref[:, c * tc + h * LANES:c * tc + (h + 1) * LANES] = out.astype(BF16)

    kinds = [tuple(_col_kind(s * tn + c * tc) for c in range(tn // tc)) for s in range(n_steps)]
    for start, stop, ks in _runs(kinds):
        @pl.when((j >= start) & (j < stop))
        def _(ks=ks):
            ys = [jnp.dot(xb_ref[...], w_ref[:, c * tc:(c + 1) * tc], preferred_element_type=F32)
                  for c in range(len(ks))]
            for c, kind in enumerate(ks):
                epilogue(c, kind, ys[c])


def _inproj_qkv(x2d, w_qkv, tabs, colscale, seq, tm=512, tn=1536, tc=256):
    t, d = x2d.shape
    tm = min(tm, seq)
    n_steps = QKV_COLS // tn
    per_seq = seq // tm
    tab_spec = pl.BlockSpec((tm, LANES), lambda i, j: (i % per_seq, 0))
    est = (2 * tm * d * 4 + 2 * tm * d * 2 + 2 * d * tn * 2 + 8 * tm * LANES * 4 + 2 * tm * tn * 2
           + 3 * tm * tn * 4)
    return pl.pallas_call(
        functools.partial(_inproj_qkv_kernel, tn=tn, tc=tc, n_steps=n_steps),
        grid=(t // tm, n_steps),
        in_specs=[
            pl.BlockSpec((tm, d), lambda i, j: (i, 0)),
            pl.BlockSpec((d, tn), lambda i, j: (0, j)),
            tab_spec, tab_spec, tab_spec, tab_spec,
            pl.BlockSpec((1, tn), lambda i, j: (0, j)),
        ],
        out_specs=[
            pl.BlockSpec((tm, tn), lambda i, j: (i, j)),
            pl.BlockSpec((tm, d), lambda i, j: (i, 0)),
        ],
        out_shape=[jax.ShapeDtypeStruct((t, QKV_COLS), BF16), jax.ShapeDtypeStruct((t, d), BF16)],
        compiler_params=pltpu.CompilerParams(
            dimension_semantics=("arbitrary", "arbitrary"), vmem_limit_bytes=_vmem_limit(est)),
        name="inproj_qkv",
    )(x2d, w_qkv, *tabs, colscale)


def _inproj_gates_kernel(x_ref, w_ref, bias_ref, g_ref, *, tc):
    n = w_ref.shape[1] // tc
    ys = [jnp.dot(x_ref[...], w_ref[:, c * tc:(c + 1) * tc], preferred_element_type=F32) for c in range(n)]
    for c in range(n):
        cols = slice(c * tc, (c + 1) * tc)
        g_ref[:, cols] = (0.5 * jnp.tanh(0.5 * (ys[c] + bias_ref[:, cols])) + 0.5).astype(BF16)


def _inproj_gates(xb, w_gate, bias, tm=2048, tn=1024, tc=256):
    t, d = xb.shape
    cols = w_gate.shape[1]
    tm = min(tm, t)
    tn = min(tn, cols)
    est = 2 * tm * d * 2 + 2 * d * tn * 2 + 2 * tm * tn * 2 + 3 * tm * tn * 4
    return pl.pallas_call(
        functools.partial(_inproj_gates_kernel, tc=tc),
        grid=(t // tm, cols // tn),
        in_specs=[
            pl.BlockSpec((tm, d), lambda i, j: (i, 0)),
            pl.BlockSpec((d, tn), lambda i, j: (0, j)),
            pl.BlockSpec((1, tn), lambda i, j: (0, j)),
        ],
        out_specs=pl.BlockSpec((tm, tn), lambda i, j: (i, j)),
        out_shape=jax.ShapeDtypeStruct((t, cols), BF16),
        compiler_params=pltpu.CompilerParams(
            dimension_semantics=("parallel", "arbitrary"), vmem_limit_bytes=_vmem_limit(est)),
        name="inproj_gates",
    )(xb, w_gate, bias)


def _softmax_parts(q, k):
    s = lax.dot_general(q, k, (((1,), (1,)), ((), ())), preferred_element_type=F32)
    p = jnp.exp2(s - jnp.max(s, axis=-1, keepdims=True))
    return p, jnp.sum(p, axis=-1, keepdims=True)


def _attn_a_kernel(q_ref, k_ref, v_ref, o_ref, *, tq, unroll):
    nq = q_ref.shape[0] // tq

    def body(qi, carry):
        r = pl.multiple_of(qi * tq, tq)
        for g in range(A_GROUP):
            sl = slice(g * HEAD_DIM, (g + 1) * HEAD_DIM)
            p, l = _softmax_parts(q_ref[pl.ds(r, tq), sl], k_ref[...])
            o = jnp.dot(p.astype(BF16), v_ref[...], preferred_element_type=F32)
            o_ref[pl.ds(r, tq), sl] = (o / l).astype(BF16)
        return carry

    lax.fori_loop(0, nq, body, 0, unroll=unroll)


def _attn_a(qkv, bsz, seq, tq=256, unroll=4):
    t = qkv.shape[0]
    gw = A_GROUP * HEAD_DIM
    tq = min(tq, seq)
    unroll = min(unroll, seq // tq)
    est = 2 * (2 * seq * gw * 2 + 2 * seq * HEAD_DIM * 2) + 4 * unroll * tq * seq * 4
    return pl.pallas_call(
        functools.partial(_attn_a_kernel, tq=tq, unroll=unroll),
        grid=(bsz, A_KV_HEADS),
        in_specs=[
            pl.BlockSpec((seq, gw), lambda b, h: (b, h)),
            pl.BlockSpec((seq, HEAD_DIM), lambda b, h: (b, OFF_A_K // HEAD_DIM + h)),
            pl.BlockSpec((seq, HEAD_DIM), lambda b, h: (b, OFF_A_V // HEAD_DIM + h)),
        ],
        out_specs=pl.BlockSpec((seq, gw), lambda b, h: (b, h)),
        out_shape=jax.ShapeDtypeStruct((t, A_WIDTH), BF16),
        compiler_params=pltpu.CompilerParams(
            dimension_semantics=("parallel", "parallel"), vmem_limit_bytes=_vmem_limit(est)),
        name="attn_a",
    )(qkv, qkv, qkv)


def _attn_b_kernel(lam_ref, sub_ref, q_ref, k_ref, v_ref, o_ref, *, tq, lam_init, unroll):
    nq = q_ref.shape[0] // tq
    lp = lam_ref[...]
    lam = (jnp.exp(jnp.sum(lp[0:1] * lp[1:2], axis=-1, keepdims=True))
           - jnp.exp(jnp.sum(lp[2:3] * lp[3:4], axis=-1, keepdims=True)) + lam_init)
    d = B_QK_DIM

    def body(qi, carry):
        r = pl.multiple_of(qi * tq, tq)
        p1, l1 = _softmax_parts(q_ref[pl.ds(r, tq), 0:d], k_ref[:, 0:d])
        p2, l2 = _softmax_parts(q_ref[pl.ds(r, tq), d:2 * d], k_ref[:, d:2 * d])
        o1 = jnp.dot(p1.astype(BF16), v_ref[...], preferred_element_type=F32)
        o2 = jnp.dot(p2.astype(BF16), v_ref[...], preferred_element_type=F32)
        o = o1 * (1.0 / l1) - o2 * (lam / l2)
        ms = jnp.mean(o * o, axis=-1, keepdims=True)
        o = o * lax.rsqrt(ms + RMS_EPS) * sub_ref[...] * (1.0 - lam_init)
        o_ref[pl.ds(r, tq), :] = o.astype(BF16)
        return carry

    lax.fori_loop(0, nq, body, 0, unroll=unroll)


def _attn_b(qkv, lam_p, subln, bsz, seq, lam_init, tq=256, unroll=8):
    t = qkv.shape[0]
    tq = min(tq, seq)
    unroll = min(unroll, seq // tq)
    w = 2 * B_QK_DIM
    est = 2 * 4 * seq * w * 2 + 5 * unroll * tq * seq * 4
    return pl.pallas_call(
        functools.partial(_attn_b_kernel, tq=tq, lam_init=lam_init, unroll=unroll),
        grid=(bsz, B_HEADS),
        in_specs=[
            pl.BlockSpec((4, B_QK_DIM), lambda b, h: (0, 0)),
            pl.BlockSpec((1, B_V_DIM), lambda b, h: (0, 0)),
            pl.BlockSpec((seq, w), lambda b, h: (b, OFF_B_Q // w + h)),
            pl.BlockSpec((seq, w), lambda b, h: (b, OFF_B_K // w + h)),
            pl.BlockSpec((seq, B_V_DIM), lambda b, h: (b, OFF_B_V // B_V_DIM + h)),
        ],
        out_specs=pl.BlockSpec((seq, B_V_DIM), lambda b, h: (b, h)),
        out_shape=jax.ShapeDtypeStruct((t, B_WIDTH), BF16),
        compiler_params=pltpu.CompilerParams(
            dimension_semantics=("parallel", "parallel"), vmem_limit_bytes=_vmem_limit(est)),
        name="attn_b",
    )(lam_p, subln, qkv, qkv, qkv)


def _layer_norm(v, g, b):
    mu = jnp.mean(v, axis=-1, keepdims=True)
    c = v - mu
    var = jnp.mean(c * c, axis=-1, keepdims=True)
    return c * lax.rsqrt(var + LN_EPS) * g + b


def _mixproj_kernel(oa_ref, ob_ref, g_ref, x_ref, wa_ref, wb_ref, wo_ref, lg_ref, lb_ref, wr_ref,
                    x1_ref, x1b_ref, aff_ref, *, alpha, kc):
    d = x_ref.shape[1]
    ne = aff_ref.shape[1]
    mix = None
    for n in range(d // kc):
        cols = slice(n * kc, (n + 1) * kc)
        ya = jnp.dot(oa_ref[...], wa_ref[:, cols], preferred_element_type=F32)
        yb = jnp.dot(ob_ref[...], wb_ref[:, cols], preferred_element_type=F32)
        merged = (g_ref[:, cols].astype(F32) * ya
                  + g_ref[:, d + n * kc:d + (n + 1) * kc].astype(F32) * yb).astype(BF16)
        part = jnp.dot(merged, wo_ref[cols, :], preferred_element_type=F32)
        mix = part if mix is None else mix + part
    x1 = _layer_norm(alpha * x_ref[...] + mix, lg_ref[...], lb_ref[...])
    x1_ref[...] = x1
    hi = x1.astype(BF16)
    x1b_ref[...] = hi
    lo = (x1 - hi.astype(F32)).astype(BF16)
    both = jnp.dot(hi, wr_ref[...], preferred_element_type=F32)
    low = jnp.dot(lo, wr_ref[:, 0:ne], preferred_element_type=F32)
    logits = both[:, 0:ne] + (both[:, ne:2 * ne] + low)
    e = jnp.exp(logits - jnp.max(logits, axis=-1, keepdims=True))
    aff_ref[...] = e / jnp.sum(e, axis=-1, keepdims=True)


def _mixproj(oa, ob, g, x2d, wa, wb, wo, ln_g, ln_b, wr2, alpha, tm=512, kc=512):
    t, d = x2d.shape
    ne = wr2.shape[1] // 2
    tm = min(tm, t)
    kc = min(kc, d)
    const = lambda shape: pl.BlockSpec(shape, lambda i: (0, 0), pipeline_mode=pl.Buffered(1))
    row = lambda width: pl.BlockSpec((tm, width), lambda i: (i, 0))
    est = ((A_WIDTH + B_WIDTH + d) * d * 2 + 2 * tm * (A_WIDTH + B_WIDTH) * 2 + 2 * tm * 2 * d * 2
           + 2 * tm * d * 4 + 2 * tm * d * 6 + 6 * tm * d * 4 + d * LANES * 2)
    return pl.pallas_call(
        functools.partial(_mixproj_kernel, alpha=alpha, kc=kc),
        grid=(t // tm,),
        in_specs=[row(A_WIDTH), row(B_WIDTH), row(2 * d), row(d),
                  const((A_WIDTH, d)), const((B_WIDTH, d)), const((d, d)),
                  const((1, d)), const((1, d)), const((d, 2 * ne))],
        out_specs=[row(d), row(d), row(ne)],
        out_shape=[
            jax.ShapeDtypeStruct((t, d), F32),
            jax.ShapeDtypeStruct((t, d), BF16),
            jax.ShapeDtypeStruct((t, ne), F32),
        ],
        compiler_params=pltpu.CompilerParams(
            dimension_semantics=("parallel",), vmem_limit_bytes=_vmem_limit(est)),
        name="mixproj",
    )(oa, ob, g, x2d, wa, wb, wo, ln_g, ln_b, wr2)


def _exclusive_prefix_chunks(mask_chunks, tri):
    out = []
    offset = None
    for m in mask_chunks:
        mf = m.astype(F32)
        incl = jnp.dot(m.astype(BF16), tri, preferred_element_type=F32)
        excl = incl - mf
        out.append(excl if offset is None else excl + offset)
        total = incl[:, LANES - 1:LANES]
        offset = total if offset is None else offset + total
    return out


def _topk_kernel(aff_ref, pos_ref, *, cap):
    ne, seq = aff_ref.shape
    aff = aff_ref[...]

    def enough(cand):
        cnt = jnp.sum((aff >= pltpu.bitcast(cand, F32)).astype(jnp.int32), axis=-1, keepdims=True)
        return cnt >= cap

    def search(i, thr):
        hi = lax.shift_left(jnp.int32(1), 30 - 2 * i)
        lo = lax.shift_left(jnp.int32(1), 29 - 2 * i)
        c3, c2, c1 = thr | hi | lo, thr | hi, thr | lo
        return jnp.where(enough(c3), c3, jnp.where(enough(c2), c2, jnp.where(enough(c1), c1, thr)))

    thr = lax.fori_loop(0, 15, search, jnp.zeros((ne, 1), jnp.int32))
    thr = jnp.where(enough(thr | 1), thr | 1, thr)
    gt = aff >= pltpu.bitcast(thr + 1, F32)
    eq = (aff >= pltpu.bitcast(thr, F32)) & jnp.logical_not(gt)
    need = cap - jnp.sum(gt.astype(jnp.int32), axis=-1, keepdims=True)
    tri = (lax.broadcasted_iota(jnp.int32, (LANES, LANES), 0)
           <= lax.broadcasted_iota(jnp.int32, (LANES, LANES), 1)).astype(BF16)
    chunks = [slice(c * LANES, (c + 1) * LANES) for c in range(seq // LANES)]
    eq_rank = _exclusive_prefix_chunks([eq[:, c] for c in chunks], tri)
    sel = [gt[:, c] | (eq[:, c] & (r < need.astype(F32))) for c, r in zip(chunks, eq_rank)]
    slot = _exclusive_prefix_chunks(sel, tri)
    for c, s, p in zip(chunks, sel, slot):
        pos_ref[:, c] = jnp.where(s, p.astype(jnp.int32), -1)


def _topk(aff_t, cap):
    bsz, ne, seq = aff_t.shape
    return pl.pallas_call(
        functools.partial(_topk_kernel, cap=cap),
        grid=(bsz,),
        in_specs=[pl.BlockSpec((None, ne, seq), lambda b: (b, 0, 0))],
        out_specs=pl.BlockSpec((None, ne, seq), lambda b: (b, 0, 0)),
        out_shape=jax.ShapeDtypeStruct((bsz, ne, seq), jnp.int32),
        compiler_params=pltpu.CompilerParams(dimension_semantics=("parallel",)),
        name="topk",
    )(aff_t)


def _gather_kernel(pos_ref, aff_ref, x_ref, xg_ref, gs_ref, *, cap):
    seq = x_ref.shape[0]
    slot = lax.broadcasted_iota(jnp.int32, (cap, seq), 0)
    for e in range(pos_ref.shape[0]):
        hit = pos_ref[e] == slot
        xg = jnp.dot(jnp.where(hit, 1.0, 0.0).astype(BF16), x_ref[...], preferred_element_type=F32)
        xg_ref[e] = xg.astype(BF16)
        gs_ref[e] = jnp.sum(jnp.where(hit, aff_ref[e], 0.0), axis=-1, keepdims=True)


def _gather(pos4, aff4, x1b, cap, eg=8):
    bsz, ne, _, seq = pos4.shape
    d = x1b.shape[1]
    eg = min(eg, ne)
    row_spec = pl.BlockSpec((None, eg, 1, seq), lambda b, e: (b, e, 0, 0))
    est = 2 * seq * d * 2 + 2 * eg * cap * d * 2 + 4 * cap * seq * 4 + 2 * cap * d * 4
    return pl.pallas_call(
        functools.partial(_gather_kernel, cap=cap),
        grid=(bsz, ne // eg),
        in_specs=[row_spec, row_spec, pl.BlockSpec((seq, d), lambda b, e: (b, 0))],
        out_specs=[
            pl.BlockSpec((eg, cap, d), lambda b, e: (e, b, 0)),
            pl.BlockSpec((eg, cap, 1), lambda b, e: (e, b, 0)),
        ],
        out_shape=[
            jax.ShapeDtypeStruct((ne, bsz * cap, d), BF16),
            jax.ShapeDtypeStruct((ne, bsz * cap, 1), F32),
        ],
        compiler_params=pltpu.CompilerParams(
            dimension_semantics=("parallel", "arbitrary"), vmem_limit_bytes=_vmem_limit(est)),
        name="gather",
    )(pos4, aff4, x1b)


def _experts_kernel(xg_ref, gs_ref, wg_ref, wu_ref, wd_ref, og_ref, acc_ref):
    f = pl.program_id(2)

    x = xg_ref[...]
    hg = jnp.dot(x, wg_ref[...].astype(BF16), preferred_element_type=F32)
    hu = jnp.dot(x, wu_ref[...].astype(BF16), preferred_element_type=F32)
    hid = hg * (0.5 * jnp.tanh(0.5 * hg) + 0.5) * hu
    down = jnp.dot(hid.astype(BF16), wd_ref[...].astype(BF16), preferred_element_type=F32)
    acc_ref[...] = jnp.where(f == 0, 0.0, acc_ref[...]) + down

    @pl.when(f == pl.num_programs(2) - 1)
    def _():
        og_ref[...] = (acc_ref[...] * gs_ref[...]).astype(BF16)


def _experts(xg, gs, w_gate, w_up, w_down, tmx=1024, tf=512):
    ne, rows, d = xg.shape
    ff = w_gate.shape[2]
    tmx = min(tmx, rows)
    tf = min(tf, ff)
    est = (2 * tmx * d * 2 + 2 * tmx * LANES * 4 + 2 * 3 * d * tf * 4 + 2 * tmx * d * 2 + tmx * d * 4
           + 3 * d * tf * 2 + 4 * tmx * tf * 4 + tmx * d * 4)
    return pl.pallas_call(
        _experts_kernel,
        grid=(ne, rows // tmx, ff // tf),
        in_specs=[
            pl.BlockSpec((None, tmx, d), lambda e, m, f: (e, m, 0)),
            pl.BlockSpec((None, tmx, 1), lambda e, m, f: (e, m, 0)),
            pl.BlockSpec((None, d, tf), lambda e, m, f: (e, 0, f)),
            pl.BlockSpec((None, d, tf), lambda e, m, f: (e, 0, f)),
            pl.BlockSpec((None, tf, d), lambda e, m, f: (e, f, 0)),
        ],
        out_specs=pl.BlockSpec((None, tmx, d), lambda e, m, f: (e, m, 0)),
        out_shape=jax.ShapeDtypeStruct((ne, rows, d), BF16),
        scratch_shapes=[pltpu.VMEM((tmx, d), F32)],
        compiler_params=pltpu.CompilerParams(
            dimension_semantics=("parallel", "parallel", "arbitrary"), vmem_limit_bytes=_vmem_limit(est)),
        name="experts",
    )(xg, gs, w_gate, w_up, w_down)


def _combine_kernel(pos_ref, og_ref, x1_ref, lg_ref, lb_ref, o_ref, *, cap, alpha):
    ts = x1_ref.shape[0]
    slot = lax.broadcasted_iota(jnp.int32, (cap, ts), 0)
    y = None
    for e in range(og_ref.shape[0]):
        onehot = jnp.where(pos_ref[e] == slot, 1.0, 0.0).astype(BF16)
        part = lax.dot_general(onehot, og_ref[e], (((0,), (0,)), ((), ())), preferred_element_type=F32)
        y = part if y is None else y + part
    o_ref[...] = _layer_norm(alpha * x1_ref[...] + y, lg_ref[...], lb_ref[...])


def _combine(pos4, og, x1, ln_g, ln_b, cap, alpha, ts=256):
    bsz, ne, _, seq = pos4.shape
    t, d = x1.shape
    ts = min(ts, seq)
    nth = seq // ts
    est = 2 * ne * cap * d * 2 + 4 * ts * d * 4 + 6 * ts * d * 4 + 4 * cap * ts * 4
    return pl.pallas_call(
        functools.partial(_combine_kernel, cap=cap, alpha=alpha),
        grid=(bsz, nth),
        in_specs=[
            pl.BlockSpec((None, ne, 1, ts), lambda b, h: (b, 0, 0, h)),
            pl.BlockSpec((ne, cap, d), lambda b, h: (0, b, 0)),
            pl.BlockSpec((ts, d), lambda b, h: (b * nth + h, 0)),
            pl.BlockSpec((1, d), lambda b, h: (0, 0)),
            pl.BlockSpec((1, d), lambda b, h: (0, 0)),
        ],
        out_specs=pl.BlockSpec((ts, d), lambda b, h: (b * nth + h, 0)),
        out_shape=jax.ShapeDtypeStruct((t, d), F32),
        compiler_params=pltpu.CompilerParams(
            dimension_semantics=("parallel", "arbitrary"), vmem_limit_bytes=_vmem_limit(est)),
        name="combine",
    )(pos4, og, x1, ln_g, ln_b)


def kernel(x, w_in, b_gate, a_q_norm, a_k_norm, b_lambda, b_subln, w_a_proj, w_b_proj, w_o, ln1_g, ln1_b,
           w_router, w_gate, w_up, w_down, ln2_g, ln2_b):
    bsz, seq, d = x.shape
    depth = w_in.shape[0]
    ne = w_router.shape[2]
    cap = CAPACITY_FACTOR * seq // ne
    alpha = (2.0 * depth) ** 0.25
    qscale = HEAD_DIM ** -0.5 * LOG2E
    tabs = _rope_tables(seq)
    ones = lambda n: jnp.ones((n,), F32)

    x2d = x.reshape(bsz * seq, d)
    for l in range(depth):
        lam_init = 0.8 - 0.6 * math.exp(-0.3 * l)
        colscale = jnp.concatenate([
            jnp.tile(a_q_norm[l][_PERM_A], A_Q_HEADS) * qscale, jnp.tile(a_k_norm[l][_PERM_A], A_KV_HEADS),
            ones(COL_A_V), ones(COL_B_Q) * qscale, ones(COL_B_K + COL_B_V)])[None, :]
        w_qkv, w_g = _prep_w(w_in[l])
        qkv, xb = _inproj_qkv(x2d, w_qkv, tabs, colscale, seq)
        gates = _inproj_gates(xb, w_g, b_gate[l][None, :])
        oa = _attn_a(qkv, bsz, seq)
        ob = _attn_b(qkv, b_lambda[l], b_subln[l][None, :], bsz, seq, lam_init)
        wr_hi = w_router[l].astype(BF16)
        wr_lo = (w_router[l] - wr_hi.astype(F32)).astype(BF16)
        x1, x1b, aff = _mixproj(oa, ob, gates, x2d, w_a_proj[l].astype(BF16), w_b_proj[l].astype(BF16),
                                w_o[l].astype(BF16), ln1_g[l][None, :], ln1_b[l][None, :],
                                jnp.concatenate([wr_hi, wr_lo], axis=1), alpha)
        aff_t = jnp.swapaxes(aff.reshape(bsz, seq, ne), 1, 2)
        pos = _topk(aff_t, cap)
        pos4 = pos.reshape(bsz, ne, 1, seq)
        xg, gs = _gather(pos4, aff_t.reshape(bsz, ne, 1, seq), x1b, cap)
        og = _experts(xg, gs, w_gate[l], w_up[l], w_down[l])
        x2d = _combine(pos4, og, x1, ln2_g[l][None, :], ln2_b[l][None, :], cap, alpha)
    return x2d.reshape(bsz, seq, d)
```
